```python
import math, functools
import jax, jax.numpy as jnp
from jax import lax
import numpy as np

D_MODEL = 2048
BATCH = 2
SEQ = 4096
DEPTH = 2
DEC_BATCH = 128
DEC_SEQ = 4
PAST_LEN = 2048
PAGE_SIZE = 128

N_HEADS = 16
N_KV_HEADS = 8
HEAD_DIM = 128
KV_GROUP = N_HEADS // N_KV_HEADS
Q_BLOCK = 128
ATTN_WIDTH = N_HEADS * HEAD_DIM
KV_WIDTH = N_KV_HEADS * HEAD_DIM
SSM_EXPAND = 2
D_INNER = SSM_EXPAND * D_MODEL
SSM_HEAD_DIM = 64
SSM_HEADS = D_INNER // SSM_HEAD_DIM
SSM_GROUPS = 8
SSM_HEADS_PER_GROUP = SSM_HEADS // SSM_GROUPS
D_STATE = 128
CONV_W = 4
CONV_DIM = D_INNER + 2 * SSM_GROUPS * D_STATE
SSM_CHUNK = 128
D_FF = 256 * ((8 * D_MODEL // 3 + 255) // 256)
PLE_DIM = 256
EPS = 1e-6

IN_SIZES = (ATTN_WIDTH, KV_WIDTH, KV_WIDTH, N_HEADS, D_INNER, CONV_DIM, SSM_HEADS, D_MODEL, D_MODEL)
IN_SPLITS = tuple(int(s) for s in np.cumsum(IN_SIZES)[:-1])
N_IN = int(sum(IN_SIZES))

kernel_name = 'fox_ssd_gated_hybrid_step'


def rmsnorm(x, g):
    xf = x.astype(jnp.float32)
    y = xf * lax.rsqrt(jnp.mean(xf * xf, axis=-1, keepdims=True) + EPS)
    return (y * g.astype(jnp.float32)).astype(x.dtype)


def swiglu(x, w_in, w_out):
    a, b = jnp.split(x @ w_in, 2, axis=-1)
    return (jax.nn.silu(a) * b) @ w_out


def attend_prompt(q, k, v, logf):
    bsz, seq = q.shape[:2]
    nb = seq // Q_BLOCK
    scale = HEAD_DIM ** -0.5
    c = jnp.cumsum(logf, axis=1).reshape(bsz, seq, N_KV_HEADS, KV_GROUP)
    c_keys = jnp.transpose(c, (0, 2, 3, 1))
    qb = q.reshape(bsz, nb, Q_BLOCK, N_KV_HEADS, KV_GROUP, HEAD_DIM).transpose(1, 0, 2, 3, 4, 5)
    cb = c.reshape(bsz, nb, Q_BLOCK, N_KV_HEADS, KV_GROUP).transpose(1, 0, 3, 4, 2)
    kpos = jnp.arange(seq)

    def block(args):
        q_blk, c_blk, start = args
        s = jnp.einsum('bqkgd,bskd->bkgqs', q_blk, k, preferred_element_type=jnp.float32) * scale
        s = s + (c_blk[..., :, None] - c_keys[..., None, :])
        qpos = start + jnp.arange(Q_BLOCK)
        s = jnp.where(kpos[None, :] <= qpos[:, None], s, -jnp.inf)
        w = jax.nn.softmax(s, axis=-1).astype(v.dtype)
        return jnp.einsum('bkgqs,bskd->bqkgd', w, v)

    o = lax.map(block, (qb, cb, jnp.arange(nb) * Q_BLOCK))
    return o.transpose(1, 0, 2, 3, 4, 5).reshape(bsz, seq, ATTN_WIDTH)


def attend_sample(q, k, v, logf, k_past, v_past, logf_past):
    db, t = q.shape[:2]
    past = k_past.shape[1]
    scale = HEAD_DIM ** -0.5
    qg = q.reshape(db, t, N_KV_HEADS, KV_GROUP, HEAD_DIM)
    lp = logf_past.astype(jnp.float32)
    r_past = jnp.sum(lp, axis=1, keepdims=True) - jnp.cumsum(lp, axis=1)
    r_past = r_past.reshape(db, past, N_KV_HEADS, KV_GROUP).transpose(0, 2, 3, 1)
    cn = jnp.cumsum(logf, axis=1).reshape(db, t, N_KV_HEADS, KV_GROUP).transpose(0, 2, 3, 1)
    s_past = jnp.einsum('bqkgd,bskd->bkgqs', qg, k_past, preferred_element_type=jnp.float32) * scale
    s_past = s_past + cn[..., :, None] + r_past[..., None, :]
    s_new = jnp.einsum('bqkgd,bskd->bkgqs', qg, k, preferred_element_type=jnp.float32) * scale
    s_new = s_new + cn[..., :, None] - cn[..., None, :]
    idx = jnp.arange(t)
    s_new = jnp.where(idx[None, :] <= idx[:, None], s_new, -jnp.inf)
    w = jax.nn.softmax(jnp.concatenate([s_past, s_new], axis=-1), axis=-1).astype(v.dtype)
    o = (jnp.einsum('bkgqs,bskd->bqkgd', w[..., :past], v_past)
         + jnp.einsum('bkgqs,bskd->bqkgd', w[..., past:], v))
    return o.reshape(db, t, ATTN_WIDTH)


def ssd(x, a, b, c, h0, chunk):
    bsz, L = x.shape[:2]
    nc = L // chunk
    x = x.reshape(bsz, nc, chunk, SSM_GROUPS, SSM_HEADS_PER_GROUP, SSM_HEAD_DIM)
    a = a.reshape(bsz, nc, chunk, SSM_GROUPS, SSM_HEADS_PER_GROUP)
    b = b.reshape(bsz, nc, chunk, SSM_GROUPS, D_STATE)
    c = c.reshape(bsz, nc, chunk, SSM_GROUPS, D_STATE)
    a_cs = jnp.cumsum(a, axis=2)
    idx = jnp.arange(chunk)
    causal = (idx[:, None] >= idx[None, :])[None, None, :, :, None, None]
    seg = a_cs[:, :, :, None] - a_cs[:, :, None, :]
    lmat = jnp.exp(jnp.where(causal, seg, -jnp.inf))
    cb = jnp.einsum('bctgn,bcsgn->bctsg', c, b)
    y_diag = jnp.einsum('bctsg,bctsgh,bcsghp->bctghp', cb, lmat, x)
    decay = jnp.exp(a_cs[:, :, -1:] - a_cs)
    states = jnp.einsum('bcsgn,bcsgh,bcsghp->bcghpn', b, decay, x)
    chunk_decay = jnp.exp(a_cs[:, :, -1])

    def step(h, inp):
        dec, st = inp
        return dec[..., None, None] * h + st, h

    h_init = h0.reshape(bsz, SSM_GROUPS, SSM_HEADS_PER_GROUP, SSM_HEAD_DIM, D_STATE)
    h_last, h_in = lax.scan(step, h_init, (jnp.moveaxis(chunk_decay, 1, 0), jnp.moveaxis(states, 1, 0)))
    h_in = jnp.moveaxis(h_in, 0, 1)
    y_off = jnp.einsum('bctgn,bcghpn,bctgh->bctghp', c, h_in, jnp.exp(a_cs))
    y = (y_diag + y_off).reshape(bsz, L, SSM_HEADS, SSM_HEAD_DIM)
    return y, h_last.reshape(bsz, SSM_HEADS, SSM_HEAD_DIM, D_STATE)


def ssm_branch(z, xbc, dt_raw, conv_hist, h0, conv_w, conv_b, dt_bias, a_log, d_skip, g_ssm):
    bsz, L = xbc.shape[:2]
    xpad = jnp.concatenate([conv_hist.astype(xbc.dtype), xbc], axis=1)
    conv = conv_b
    for j in range(CONV_W):
        conv = conv + xpad[:, j:j + L] * conv_w[j]
    xbc_act = jax.nn.silu(conv)
    xs, bm, cm = jnp.split(xbc_act, [D_INNER, D_INNER + SSM_GROUPS * D_STATE], axis=-1)
    dt = jax.nn.softplus(dt_raw.astype(jnp.float32) + dt_bias.astype(jnp.float32))
    a = -jnp.exp(a_log.astype(jnp.float32))
    xh = xs.reshape(bsz, L, SSM_HEADS, SSM_HEAD_DIM).astype(jnp.float32)
    y, h_last = ssd(xh * dt[..., None], dt * a,
                    bm.reshape(bsz, L, SSM_GROUPS, D_STATE).astype(jnp.float32),
                    cm.reshape(bsz, L, SSM_GROUPS, D_STATE).astype(jnp.float32),
                    h0.astype(jnp.float32), math.gcd(SSM_CHUNK, L))
    y = y + d_skip.astype(jnp.float32)[:, None] * xh
    y = y.reshape(bsz, L, D_INNER) * jax.nn.silu(z.astype(jnp.float32))
    yg = y.reshape(bsz, L, SSM_GROUPS, D_INNER // SSM_GROUPS)
    yg = yg * lax.rsqrt(jnp.mean(yg * yg, axis=-1, keepdims=True) + EPS)
    y = yg.reshape(bsz, L, D_INNER) * g_ssm.astype(jnp.float32)
    return y.astype(z.dtype), xpad[:, -(CONV_W - 1):], h_last


def trunk_layer(x, p_l, lw, attend, conv_hist, ssm_h0):
    h = x + 0.5 * swiglu(rmsnorm(x, lw['g_ffn1']), lw['w_ffn1_in'], lw['w_ffn1_out'])
    u = rmsnorm(h, lw['g_mix'])
    bsz, L = u.shape[:2]
    q, k, v, fl, z, xbc, dtr, ga, gb = jnp.split(u @ lw['w_in'], IN_SPLITS, axis=-1)
    q = rmsnorm(q.reshape(bsz, L, N_HEADS, HEAD_DIM), lw['g_q'])
    k = rmsnorm(k.reshape(bsz, L, N_KV_HEADS, HEAD_DIM), lw['g_k'])
    v = v.reshape(bsz, L, N_KV_HEADS, HEAD_DIM)
    logf = jax.nn.log_sigmoid(fl.astype(jnp.float32) + lw['b_f'].astype(jnp.float32))
    o_attn = attend(q, k, v, logf)
    o_ssm, conv_new, ssm_new = ssm_branch(z, xbc, dtr, conv_hist, ssm_h0, lw['conv_w'], lw['conv_b'],
                                          lw['dt_bias'], lw['a_log'], lw['d_skip'], lw['g_ssm'])
    merged = (jax.nn.sigmoid(ga) * (o_attn @ lw['w_branch_attn'])
              + jax.nn.sigmoid(gb) * (o_ssm @ lw['w_branch_ssm']))
    h = h + merged @ lw['w_out']
    h = h + 0.5 * swiglu(rmsnorm(h, lw['g_ffn2']), lw['w_ffn2_in'], lw['w_ffn2_out'])
    gate = jax.nn.sigmoid(rmsnorm(h, lw['g_ple']) @ lw['w_ple_gate'])
    h = h + gate * (p_l @ lw['w_ple_proj'])
    return h, k, v, logf, conv_new, ssm_new


def setup_inputs(seed: int = 0) -> dict:
    key = jax.random.key(seed)
    ks = iter(jax.random.split(key, 48))

    def nrm(shape, scale):
        return jax.random.normal(next(ks), shape, jnp.float32) * scale

    def unif(shape, lo, hi):
        return jax.random.uniform(next(ks), shape, jnp.float32, lo, hi)

    n_pages = PAST_LEN // PAGE_SIZE
    n_used = DEC_BATCH * n_pages
    n_pool = n_used + max(1, n_used // 4)
    inp = {}
    inp['x_prompt'] = nrm((BATCH, SEQ, D_MODEL), 1.0)
    inp['x_sample'] = nrm((DEC_BATCH, DEC_SEQ, D_MODEL), 1.0)
    inp['cache_k'] = nrm((DEPTH, n_pool, PAGE_SIZE, N_KV_HEADS, HEAD_DIM), 1.0)
    inp['cache_v'] = nrm((DEPTH, n_pool, PAGE_SIZE, N_KV_HEADS, HEAD_DIM), 1.0)
    inp['cache_logf'] = jax.nn.log_sigmoid(2.0 + nrm((DEPTH, n_pool, PAGE_SIZE, N_HEADS), 1.0))
    inp['state_ssm'] = nrm((DEPTH, DEC_BATCH, SSM_HEADS, SSM_HEAD_DIM, D_STATE), 0.5)
    inp['state_conv'] = nrm((DEPTH, DEC_BATCH, CONV_W - 1, CONV_DIM), 1.0)
    inp['page_table'] = jax.random.permutation(next(ks), n_pool)[:n_used].reshape(DEC_BATCH, n_pages).astype(jnp.int32)
    inp['p_prompt'] = nrm((DEPTH, BATCH, SEQ, PLE_DIM), 1.0)
    inp['p_sample'] = nrm((DEPTH, DEC_BATCH, DEC_SEQ, PLE_DIM), 1.0)
    inp['g_ffn1'] = 1.0 + nrm((DEPTH, D_MODEL), 0.05)
    inp['w_ffn1_in'] = nrm((DEPTH, D_MODEL, 2 * D_FF), D_MODEL ** -0.5)
    inp['w_ffn1_out'] = nrm((DEPTH, D_FF, D_MODEL), D_FF ** -0.5)
    inp['g_mix'] = 1.0 + nrm((DEPTH, D_MODEL), 0.05)
    inp['w_in'] = nrm((DEPTH, D_MODEL, N_IN), D_MODEL ** -0.5)
    inp['b_f'] = 2.0 + nrm((DEPTH, N_HEADS), 0.5)
    inp['g_q'] = 1.0 + nrm((DEPTH, HEAD_DIM), 0.05)
    inp['g_k'] = 1.0 + nrm((DEPTH, HEAD_DIM), 0.05)
    inp['conv_w'] = nrm((DEPTH, CONV_W, CONV_DIM), CONV_W ** -0.5)
    inp['conv_b'] = nrm((DEPTH, CONV_DIM), 0.02)
    dt0 = jnp.exp(unif((DEPTH, SSM_HEADS), math.log(1e-3), math.log(1e-1)))
    inp['dt_bias'] = dt0 + jnp.log(-jnp.expm1(-dt0))
    inp['a_log'] = jnp.log(unif((DEPTH, SSM_HEADS), 1.0, 16.0))
    inp['d_skip'] = 1.0 + nrm((DEPTH, SSM_HEADS), 0.05)
    inp['g_ssm'] = 1.0 + nrm((DEPTH, D_INNER), 0.05)
    inp['w_branch_attn'] = nrm((DEPTH, ATTN_WIDTH, D_MODEL), ATTN_WIDTH ** -0.5)
    inp['w_branch_ssm'] = nrm((DEPTH, D_INNER, D_MODEL), D_INNER ** -0.5)
    inp['w_out'] = nrm((DEPTH, D_MODEL, D_MODEL), D_MODEL ** -0.5)
    inp['g_ffn2'] = 1.0 + nrm((DEPTH, D_MODEL), 0.05)
    inp['w_ffn2_in'] = nrm((DEPTH, D_MODEL, 2 * D_FF), D_MODEL ** -0.5)
    inp['w_ffn2_out'] = nrm((DEPTH, D_FF, D_MODEL), D_FF ** -0.5)
    inp['g_ple'] = 1.0 + nrm((DEPTH, D_MODEL), 0.05)
    inp['w_ple_gate'] = nrm((DEPTH, D_MODEL, D_MODEL), D_MODEL ** -0.5)
    inp['w_ple_proj'] = nrm((DEPTH, PLE_DIM, D_MODEL), PLE_DIM ** -0.5)
    return inp


def reference(x_prompt, x_sample, cache_k, cache_v, cache_logf, state_ssm, state_conv, page_table,
              p_prompt, p_sample, g_ffn1, w_ffn1_in, w_ffn1_out, g_mix, w_in, b_f, g_q, g_k,
              conv_w, conv_b, dt_bias, a_log, d_skip, g_ssm, w_branch_attn, w_branch_ssm, w_out,
              g_ffn2, w_ffn2_in, w_ffn2_out, g_ple, w_ple_gate, w_ple_proj):
    db = x_sample.shape[0]
    bp = x_prompt.shape[0]
    yp, ys = x_prompt, x_sample
    kp_l, vp_l, lfp_l, cvp_l, ssp_l = [], [], [], [], []
    ks_l, vs_l, lfs_l, cvs_l, sss_l = [], [], [], [], []
    for i in range(DEPTH):
        lw = dict(g_ffn1=g_ffn1[i], w_ffn1_in=w_ffn1_in[i], w_ffn1_out=w_ffn1_out[i], g_mix=g_mix[i],
                  w_in=w_in[i], b_f=b_f[i], g_q=g_q[i], g_k=g_k[i], conv_w=conv_w[i], conv_b=conv_b[i],
                  dt_bias=dt_bias[i], a_log=a_log[i], d_skip=d_skip[i], g_ssm=g_ssm[i],
                  w_branch_attn=w_branch_attn[i], w_branch_ssm=w_branch_ssm[i], w_out=w_out[i],
                  g_ffn2=g_ffn2[i], w_ffn2_in=w_ffn2_in[i], w_ffn2_out=w_ffn2_out[i],
                  g_ple=g_ple[i], w_ple_gate=w_ple_gate[i], w_ple_proj=w_ple_proj[i])
        conv0 = jnp.zeros((bp, CONV_W - 1, CONV_DIM), yp.dtype)
        h0 = jnp.zeros((bp, SSM_HEADS, SSM_HEAD_DIM, D_STATE), jnp.float32)
        yp, kp, vp, lfp, cvp, ssp = trunk_layer(yp, p_prompt[i], lw, attend_prompt, conv0, h0)
        k_past = cache_k[i][page_table].reshape(db, -1, N_KV_HEADS, HEAD_DIM)
        v_past = cache_v[i][page_table].reshape(db, -1, N_KV_HEADS, HEAD_DIM)
        lf_past = cache_logf[i][page_table].reshape(db, -1, N_HEADS)
        attend = functools.partial(attend_sample, k_past=k_past, v_past=v_past, logf_past=lf_past)
        ys, ksm, vsm, lfs, cvs, sss = trunk_layer(ys, p_sample[i], lw, attend, state_conv[i], state_ssm[i])
        kp_l.append(kp); vp_l.append(vp); lfp_l.append(lfp); cvp_l.append(cvp); ssp_l.append(ssp)
        ks_l.append(ksm); vs_l.append(vsm); lfs_l.append(lfs); cvs_l.append(cvs); sss_l.append(sss)
    return (yp, ys,
            jnp.stack(kp_l), jnp.stack(vp_l), jnp.stack(lfp_l), jnp.stack(cvp_l), jnp.stack(ssp_l),
            jnp.stack(ks_l), jnp.stack(vs_l), jnp.stack(lfs_l), jnp.stack(cvs_l), jnp.stack(sss_l))
```

```python
import functools
import math

import numpy as np
import jax
import jax.numpy as jnp
from jax import lax
from jax.experimental import pallas as pl
from jax.experimental.pallas import tpu as pltpu

F32 = jnp.float32
BF16 = jnp.bfloat16
EPS = 1e-6
NEG_INF = float("-inf")

LANES = 128
SUBLANES = 8
VMEM_LIMIT_BYTES = 56 * 1024 * 1024

HEAD_DIM = 128
N_HEADS = 16
N_KV_HEADS = 8
KV_GROUP = N_HEADS // N_KV_HEADS
SSM_HEAD_DIM = 64
SSM_GROUPS = 8
HEADS_PER_GROUP = 8
GROUP_WIDTH = HEADS_PER_GROUP * SSM_HEAD_DIM
D_STATE = 128
CONV_W = 4
SSM_CHUNK = 128
PAGE = 128

TM = 512
TN = 512
TQ = 512
TOK8 = 8


def _cparams(*sem):
    return pltpu.CompilerParams(dimension_semantics=sem, vmem_limit_bytes=VMEM_LIMIT_BYTES)


def _sigmoid(x):
    return 1.0 / (1.0 + jnp.exp(-x))


def _silu(x):
    return x * _sigmoid(x)


def _dot(a, b):
    return jnp.dot(a, b, preferred_element_type=F32)


def _dot_nt(a, b):
    return lax.dot_general(a, b, (((1,), (1,)), ((), ())), preferred_element_type=F32)


def _dot_tn(a, b):
    return lax.dot_general(a, b, (((0,), (0,)), ((), ())), preferred_element_type=F32)


def _split3(x):
    hi = x.astype(BF16)
    r1 = x - hi.astype(F32)
    mid = r1.astype(BF16)
    lo = (r1 - mid.astype(F32)).astype(BF16)
    return hi, mid, lo


def _rmsnorm_kernel(x_ref, g_ref, o_ref):
    x = x_ref[...]
    ms = jnp.mean(x * x, axis=-1, keepdims=True)
    o_ref[...] = (x * lax.rsqrt(ms + EPS) * g_ref[...]).astype(o_ref.dtype)


def rmsnorm(x, g):
    m, d = x.shape
    return pl.pallas_call(
        _rmsnorm_kernel,
        grid=(m // TM,),
        in_specs=[pl.BlockSpec((TM, d), lambda i: (i, 0)),
                  pl.BlockSpec((1, d), lambda i: (0, 0))],
        out_specs=pl.BlockSpec((TM, d), lambda i: (i, 0)),
        out_shape=jax.ShapeDtypeStruct((m, d), BF16),
        compiler_params=_cparams("parallel"),
        name="rmsnorm",
    )(x, g.reshape(1, d))


def _swiglu_kernel(a_ref, wa_ref, wb_ref, o_ref):
    a = a_ref[...]
    o_ref[...] = (_silu(_dot(a, wa_ref[...])) * _dot(a, wb_ref[...])).astype(o_ref.dtype)


def mm_swiglu(a, w):
    m, k = a.shape
    dff = w.shape[1] // 2
    nj = dff // TN
    return pl.pallas_call(
        _swiglu_kernel,
        grid=(nj, m // TM),
        in_specs=[pl.BlockSpec((TM, k), lambda j, i: (i, 0)),
                  pl.BlockSpec((k, TN), lambda j, i: (0, j)),
                  pl.BlockSpec((k, TN), lambda j, i: (0, j + nj))],
        out_specs=pl.BlockSpec((TM, TN), lambda j, i: (i, j)),
        out_shape=jax.ShapeDtypeStruct((m, dff), BF16),
        compiler_params=_cparams("parallel", "parallel"),
        name="mm_swiglu",
    )(a, w, w)


def _residual_kernel(a_ref, w_ref, r_ref, o_ref, *, alpha):
    o_ref[...] = r_ref[...] + alpha * _dot(a_ref[...], w_ref[...])


def mm_residual(a, w, res, alpha, tn=TN):
    m, k = a.shape
    n = w.shape[1]
    return pl.pallas_call(
        functools.partial(_residual_kernel, alpha=alpha),
        grid=(n // tn, m // TM),
        in_specs=[pl.BlockSpec((TM, k), lambda j, i: (i, 0)),
                  pl.BlockSpec((k, tn), lambda j, i: (0, j)),
                  pl.BlockSpec((TM, tn), lambda j, i: (i, j))],
        out_specs=pl.BlockSpec((TM, tn), lambda j, i: (i, j)),
        out_shape=jax.ShapeDtypeStruct((m, n), F32),
        compiler_params=_cparams("parallel", "parallel"),
        name="mm_residual",
    )(a, w, res)


def _plain_kernel(a_ref, w_ref, *o_refs):
    acc = _dot(a_ref[...], w_ref[...])
    for o_ref in o_refs:
        o_ref[...] = acc.astype(o_ref.dtype)


def mm_plain(a, w, out_dtypes, tn=TN):
    m, k = a.shape
    n = w.shape[1]
    outs = pl.pallas_call(
        _plain_kernel,
        grid=(n // tn, m // TM),
        in_specs=[pl.BlockSpec((TM, k), lambda j, i: (i, 0)),
                  pl.BlockSpec((k, tn), lambda j, i: (0, j))],
        out_specs=[pl.BlockSpec((TM, tn), lambda j, i: (i, j)) for _ in out_dtypes],
        out_shape=[jax.ShapeDtypeStruct((m, n), dt) for dt in out_dtypes],
        compiler_params=_cparams("parallel", "parallel"),
        name="mm_plain",
    )(a, w)
    return outs


def _headnorm_kernel(a_ref, w_ref, g_ref, *o_refs, mult):
    acc = _dot(a_ref[...], w_ref[...])
    g = g_ref[...]
    for hh in range(acc.shape[1] // HEAD_DIM):
        sl = slice(hh * HEAD_DIM, (hh + 1) * HEAD_DIM)
        blk = acc[:, sl]
        ms = jnp.mean(blk * blk, axis=-1, keepdims=True)
        y = blk * lax.rsqrt(ms + EPS) * g
        for o_ref in o_refs:
            o_ref[:, sl] = (y * mult).astype(o_ref.dtype) if o_ref.dtype == BF16 else y


def mm_headnorm(a, w, g, out_dtypes, mult=1.0):
    m, k = a.shape
    n = w.shape[1]
    return pl.pallas_call(
        functools.partial(_headnorm_kernel, mult=mult),
        grid=(n // TN, m // TM),
        in_specs=[pl.BlockSpec((TM, k), lambda j, i: (i, 0)),
                  pl.BlockSpec((k, TN), lambda j, i: (0, j)),
                  pl.BlockSpec((1, HEAD_DIM), lambda j, i: (0, 0))],
        out_specs=[pl.BlockSpec((TM, TN), lambda j, i: (i, j)) for _ in out_dtypes],
        out_shape=[jax.ShapeDtypeStruct((m, n), dt) for dt in out_dtypes],
        compiler_params=_cparams("parallel", "parallel"),
        name="mm_headnorm",
    )(a, w, g.reshape(1, HEAD_DIM))


def _merge_kernel(a1_ref, w1_ref, a2_ref, w2_ref, ga_ref, gb_ref, o_ref):
    o_ref[...] = (_sigmoid(ga_ref[...]) * _dot(a1_ref[...], w1_ref[...])
                  + _sigmoid(gb_ref[...]) * _dot(a2_ref[...], w2_ref[...])).astype(o_ref.dtype)


def mm_merge(a1, w1, a2, w2, gates, ga_col, gb_col):
    m, k1 = a1.shape
    k2 = a2.shape[1]
    n = w1.shape[1]
    ja, jb = ga_col // TN, gb_col // TN
    return pl.pallas_call(
        _merge_kernel,
        grid=(n // TN, m // TM),
        in_specs=[pl.BlockSpec((TM, k1), lambda j, i: (i, 0)),
                  pl.BlockSpec((k1, TN), lambda j, i: (0, j)),
                  pl.BlockSpec((TM, k2), lambda j, i: (i, 0)),
                  pl.BlockSpec((k2, TN), lambda j, i: (0, j)),
                  pl.BlockSpec((TM, TN), lambda j, i: (i, j + ja)),
                  pl.BlockSpec((TM, TN), lambda j, i: (i, j + jb))],
        out_specs=pl.BlockSpec((TM, TN), lambda j, i: (i, j)),
        out_shape=jax.ShapeDtypeStruct((m, n), BF16),
        compiler_params=_cparams("parallel", "parallel"),
        name="mm_merge",
    )(a1, w1, a2, w2, gates, gates)


def _ple_kernel(a_ref, wg_ref, p_ref, wp_ref, r_ref, o_ref):
    gate = _sigmoid(_dot(a_ref[...], wg_ref[...]))
    o_ref[...] = r_ref[...] + gate * _dot(p_ref[...], wp_ref[...])


def mm_ple(a, wg, p, wp, res):
    m, k = a.shape
    kp = p.shape[1]
    n = wg.shape[1]
    return pl.pallas_call(
        _ple_kernel,
        grid=(n // TN, m // TM),
        in_specs=[pl.BlockSpec((TM, k), lambda j, i: (i, 0)),
                  pl.BlockSpec((k, TN), lambda j, i: (0, j)),
                  pl.BlockSpec((TM, kp), lambda j, i: (i, 0)),
                  pl.BlockSpec((kp, TN), lambda j, i: (0, j)),
                  pl.BlockSpec((TM, TN), lambda j, i: (i, j))],
        out_specs=pl.BlockSpec((TM, TN), lambda j, i: (i, j)),
        out_shape=jax.ShapeDtypeStruct((m, n), F32),
        compiler_params=_cparams("parallel", "parallel"),
        name="mm_ple",
    )(a, wg, p, wp, res)


def _tri_cumsum(t, x):
    hi, mid, lo = _split3(x)
    return _dot(t, hi) + _dot(t, mid) + _dot(t, lo)


def _gates_kernel(x_ref, bias_ref, alog_ref, tseq_ref, tchk_ref,
                  dt_ref, lf_ref, c_ref, acs_ref, carry_ref, *, blocks_per_seq):
    i = pl.program_id(0)
    xb = x_ref[...] + bias_ref[...]
    t = jnp.log1p(jnp.exp(-jnp.abs(xb)))
    sp = jnp.maximum(xb, 0.0) + t
    ls = jnp.minimum(xb, 0.0) - t
    dt_ref[...] = sp
    lf_ref[...] = ls
    a = sp * (-jnp.exp(alog_ref[...]))
    acs_ref[...] = _tri_cumsum(tchk_ref[...], a)

    @pl.when(i % blocks_per_seq == 0)
    def _():
        carry_ref[...] = jnp.zeros_like(carry_ref)

    c = _tri_cumsum(tseq_ref[...], ls) + carry_ref[...]
    c_ref[...] = c
    carry_ref[...] = c[TM - 1:TM, :]


def gates(x, bias, alog, seq_len, chunk):
    m = x.shape[0]
    idx = np.arange(TM)
    low = idx[:, None] >= idx[None, :]
    tseq = jnp.asarray(low & (idx[:, None] // seq_len == idx[None, :] // seq_len), BF16)
    tchk = jnp.asarray(low & (idx[:, None] // chunk == idx[None, :] // chunk), BF16)
    row = pl.BlockSpec((TM, LANES), lambda i: (i, 0))
    vec = pl.BlockSpec((1, LANES), lambda i: (0, 0))
    tri = pl.BlockSpec((TM, TM), lambda i: (0, 0))
    return pl.pallas_call(
        functools.partial(_gates_kernel, blocks_per_seq=max(1, seq_len // TM)),
        grid=(m // TM,),
        in_specs=[row, vec, vec, tri, tri],
        out_specs=[row, row, row, row],
        out_shape=[jax.ShapeDtypeStruct((m, LANES), F32)] * 4,
        scratch_shapes=[pltpu.VMEM((1, LANES), F32)],
        compiler_params=_cparams("arbitrary"),
        name="gates",
    )(x, bias, alog, tseq, tchk)


def _attn_prompt_kernel(q_ref, k_ref, v_ref, ct_ref, o_ref):
    qi = pl.program_id(2)
    q0 = pl.multiple_of(qi * TQ, TQ)
    row = lax.broadcasted_iota(jnp.int32, (TQ, TQ), 0)
    col = lax.broadcasted_iota(jnp.int32, (TQ, TQ), 1)
    causal = col <= row
    for g in range(KV_GROUP):
        q = q_ref[:, g * HEAD_DIM:(g + 1) * HEAD_DIM]
        cref = ct_ref[0, 0, g:g + 1, pl.ds(q0, LANES)][:, 0:1]

        def tile(ks, m, l, acc, masked):
            kb = k_ref[pl.ds(ks, TQ), :]
            vb = v_ref[pl.ds(ks, TQ), :]
            s = _dot_nt(q, kb) + (cref - ct_ref[0, 0, g:g + 1, pl.ds(ks, TQ)])
            if masked:
                s = jnp.where(causal, s, NEG_INF)
            m_new = jnp.maximum(m, jnp.max(s, axis=-1, keepdims=True))
            p = jnp.exp(s - m_new)
            alpha = jnp.exp(m - m_new)
            l = alpha * l + jnp.sum(p, axis=-1, keepdims=True)
            acc = alpha * acc + _dot(p.astype(BF16), vb)
            return m_new, l, acc

        def body(ki, carry):
            return tile(pl.multiple_of(ki * TQ, TQ), *carry, masked=False)

        init = (jnp.full((TQ, 1), NEG_INF, F32), jnp.zeros((TQ, 1), F32),
                jnp.zeros((TQ, HEAD_DIM), F32))
        m, l, acc = lax.fori_loop(0, qi, body, init)
        m, l, acc = tile(q0, m, l, acc, masked=True)
        o_ref[:, g * HEAD_DIM:(g + 1) * HEAD_DIM] = (acc / l).astype(o_ref.dtype)


def attn_prompt(q, k, v, ct, bsz, seq):
    nq = seq // TQ
    qw = KV_GROUP * HEAD_DIM
    return pl.pallas_call(
        _attn_prompt_kernel,
        grid=(bsz, N_KV_HEADS, nq),
        in_specs=[pl.BlockSpec((TQ, qw), lambda b, h, i: (b * nq + i, h)),
                  pl.BlockSpec((seq, HEAD_DIM), lambda b, h, i: (b, h)),
                  pl.BlockSpec((seq, HEAD_DIM), lambda b, h, i: (b, h)),
                  pl.BlockSpec((1, 1, SUBLANES, seq), lambda b, h, i: (b, h, 0, 0))],
        out_specs=pl.BlockSpec((TQ, qw), lambda b, h, i: (b * nq + i, h)),
        out_shape=jax.ShapeDtypeStruct((bsz * seq, N_HEADS * HEAD_DIM), BF16),
        compiler_params=_cparams("parallel", "parallel", "arbitrary"),
        name="attn_prompt",
    )(q, k, v, ct)


def _expand_heads(a):
    lane = lax.broadcasted_iota(jnp.int32, (a.shape[0], LANES), 1)
    blocks = [jnp.where(lane < SSM_HEAD_DIM, a[:, 2 * jj:2 * jj + 1], a[:, 2 * jj + 1:2 * jj + 2])
              for jj in range(HEADS_PER_GROUP // 2)]
    return jnp.concatenate(blocks, axis=1)


def _conv_silu(cur_ref, prev_ref, w_ref, b_ref):
    cur = cur_ref[...]
    full = jnp.concatenate([prev_ref[...], cur], axis=0)
    rows = cur.shape[0]
    out = b_ref[...]
    for j in range(CONV_W):
        lo = SUBLANES - (CONV_W - 1) + j
        out = out + full[lo:lo + rows] * w_ref[j:j + 1, :]
    prev_ref[...] = cur[rows - SUBLANES:rows]
    return _silu(out)


def _ssd_prompt_kernel(x_ref, b_ref, c_ref, z_ref, dt_ref, acs_ref, acst_ref,
                       wx_ref, wb_ref, wc_ref, bx_ref, bb_ref, bc_ref, dsk_ref, gs_ref,
                       o_ref, st_ref, h_ref, px_ref, pb_ref, pc_ref):
    ci = pl.program_id(2)
    q = SSM_CHUNK

    @pl.when(ci == 0)
    def _():
        h_ref[...] = jnp.zeros_like(h_ref)
        px_ref[...] = jnp.zeros_like(px_ref)
        pb_ref[...] = jnp.zeros_like(pb_ref)
        pc_ref[...] = jnp.zeros_like(pc_ref)

    xs = _conv_silu(x_ref, px_ref, wx_ref, bx_ref)
    bm = _conv_silu(b_ref, pb_ref, wb_ref, bb_ref)
    cm = _conv_silu(c_ref, pc_ref, wc_ref, bc_ref)
    dt8 = dt_ref[0, 0]
    acs8 = acs_ref[0, 0]
    acst = acst_ref[0, 0]
    dtx = _expand_heads(dt8)
    acsx = _expand_heads(acs8)
    acs_end = acsx[q - 1:q, :]
    xdt = xs * dtx
    cmb = cm.astype(BF16)
    bmb = bm.astype(BF16)
    cb = _dot_nt(cmb, bmb)
    row = lax.broadcasted_iota(jnp.int32, (q, q), 0)
    col = lax.broadcasted_iota(jnp.int32, (q, q), 1)
    causal = row >= col
    lane = lax.broadcasted_iota(jnp.int32, (q, LANES), 1)
    yblocks = []
    for jj in range(HEADS_PER_GROUP // 2):
        xblk = xdt[:, jj * LANES:(jj + 1) * LANES]
        acc = None
        for e in range(2):
            j = 2 * jj + e
            seg = acs8[:, j:j + 1] - acst[j:j + 1, :]
            lmat = jnp.exp(jnp.where(causal, seg, NEG_INF))
            mh = (cb * lmat).astype(BF16)
            keep = (lane < SSM_HEAD_DIM) if e == 0 else (lane >= SSM_HEAD_DIM)
            part = _dot(mh, jnp.where(keep, xblk, 0.0).astype(BF16))
            acc = part if acc is None else acc + part
        yblocks.append(acc)
    y_diag = jnp.concatenate(yblocks, axis=1)
    hprev = h_ref[...]
    y_off = _dot_nt(cmb, hprev.astype(BF16)) * jnp.exp(acsx)
    xw = (xdt * jnp.exp(acs_end - acsx)).astype(BF16)
    states = _dot_tn(xw, bmb)
    for j in range(HEADS_PER_GROUP):
        sl = slice(j * SSM_HEAD_DIM, (j + 1) * SSM_HEAD_DIM)
        h_ref[sl, :] = hprev[sl, :] * jnp.exp(acst[j:j + 1, q - 1:q]) + states[sl, :]
    y = y_diag + y_off + dsk_ref[...] * xs
    y = y * _silu(z_ref[...])
    ms = jnp.mean(y * y, axis=-1, keepdims=True)
    o_ref[...] = (y * lax.rsqrt(ms + EPS) * gs_ref[...]).astype(o_ref.dtype)

    @pl.when(ci == pl.num_programs(2) - 1)
    def _():
        st_ref[0] = h_ref[...]


def ssd_prompt(zx, z_col, xbc_col, dtg, acsg, acstg, conv_w, conv_b, dskx, gs, bsz, seq):
    nc = seq // SSM_CHUNK
    d_inner = SSM_GROUPS * GROUP_WIDTH
    jz = z_col // GROUP_WIDTH
    jx = xbc_col // GROUP_WIDTH
    jb = (xbc_col + d_inner) // D_STATE
    jc = jb + SSM_GROUPS
    wjb = d_inner // D_STATE
    wjc = wjb + SSM_GROUPS
    q = SSM_CHUNK
    rowmap = lambda off: (lambda b, g, c: (b * nc + c, off + g))
    wmap = lambda off: (lambda b, g, c: (0, off + g))
    small = pl.BlockSpec((1, 1, q, HEADS_PER_GROUP), lambda b, g, c: (b, g, c, 0))
    in_specs = [
        pl.BlockSpec((q, GROUP_WIDTH), rowmap(jx)),
        pl.BlockSpec((q, D_STATE), rowmap(jb)),
        pl.BlockSpec((q, D_STATE), rowmap(jc)),
        pl.BlockSpec((q, GROUP_WIDTH), rowmap(jz)),
        small, small,
        pl.BlockSpec((1, 1, HEADS_PER_GROUP, q), lambda b, g, c: (b, g, 0, c)),
        pl.BlockSpec((CONV_W, GROUP_WIDTH), wmap(0)),
        pl.BlockSpec((CONV_W, D_STATE), wmap(wjb)),
        pl.BlockSpec((CONV_W, D_STATE), wmap(wjc)),
        pl.BlockSpec((1, GROUP_WIDTH), wmap(0)),
        pl.BlockSpec((1, D_STATE), wmap(wjb)),
        pl.BlockSpec((1, D_STATE), wmap(wjc)),
        pl.BlockSpec((1, GROUP_WIDTH), wmap(0)),
        pl.BlockSpec((1, GROUP_WIDTH), wmap(0)),
    ]
    return pl.pallas_call(
        _ssd_prompt_kernel,
        grid=(bsz, SSM_GROUPS, nc),
        in_specs=in_specs,
        out_specs=[pl.BlockSpec((q, GROUP_WIDTH), lambda b, g, c: (b * nc + c, g)),
                   pl.BlockSpec((1, GROUP_WIDTH, D_STATE), lambda b, g, c: (b, g, 0))],
        out_shape=[jax.ShapeDtypeStruct((bsz * seq, d_inner), BF16),
                   jax.ShapeDtypeStruct((bsz, d_inner, D_STATE), F32)],
        scratch_shapes=[pltpu.VMEM((GROUP_WIDTH, D_STATE), F32),
                        pltpu.VMEM((SUBLANES, GROUP_WIDTH), F32),
                        pltpu.VMEM((SUBLANES, D_STATE), F32),
                        pltpu.VMEM((SUBLANES, D_STATE), F32)],
        compiler_params=_cparams("parallel", "parallel", "arbitrary"),
        name="ssd_prompt",
    )(zx, zx, zx, zx, dtg, acsg, acstg, conv_w, conv_w, conv_w, conv_b, conv_b, conv_b, dskx, gs)


def _conv_sample_kernel(x_ref, w_ref, b_ref, o_ref):
    x = x_ref[...]
    out = b_ref[...]
    for j in range(CONV_W):
        shift = CONV_W - 1 - j
        xs = x if shift == 0 else pltpu.roll(x, shift, 0)
        out = out + xs * w_ref[j:j + 1, :]
    o_ref[...] = _silu(out)


def conv_sample(x8, conv_w, conv_b):
    m, c = x8.shape
    tr, tc = 256, 1024
    return pl.pallas_call(
        _conv_sample_kernel,
        grid=(m // tr, c // tc),
        in_specs=[pl.BlockSpec((tr, tc), lambda i, j: (i, j)),
                  pl.BlockSpec((CONV_W, tc), lambda i, j: (0, j)),
                  pl.BlockSpec((1, tc), lambda i, j: (0, j))],
        out_specs=pl.BlockSpec((tr, tc), lambda i, j: (i, j)),
        out_shape=jax.ShapeDtypeStruct((m, c), F32),
        compiler_params=_cparams("parallel", "parallel"),
        name="conv_sample",
    )(x8, conv_w, conv_b)


def _ssd_sample_kernel(h_ref, xa_ref, z_ref, xt_ref, rp_ref, dt_ref, acs_ref, acst_ref,
                       dsk_ref, gs_ref, o_ref, hn_ref, y_ref, *, n_new):
    d_inner = SSM_GROUPS * GROUP_WIDTH
    xa = xa_ref[0]
    dt8 = dt_ref[0]
    acs8 = acs_ref[0]
    acst = acst_ref[0]
    rp = rp_ref[0]
    wd = jnp.exp(rp[1:2, :] - rp[0:1, :]) * rp[2:3, :]
    xw = xt_ref[0] * wd
    cdec = jnp.exp(rp[1:2, :])
    row = lax.broadcasted_iota(jnp.int32, (TOK8, TOK8), 0)
    col = lax.broadcasted_iota(jnp.int32, (TOK8, TOK8), 1)
    causal = row >= col
    dsk = dsk_ref[...]
    for g in range(SSM_GROUPS):
        bg = xa[:, d_inner + g * D_STATE:d_inner + (g + 1) * D_STATE]
        cg = xa[:, d_inner + (SSM_GROUPS + g) * D_STATE:d_inner + (SSM_GROUPS + g + 1) * D_STATE]
        cgb = cg.astype(BF16)
        cb = _dot_nt(cgb, bg.astype(BF16))
        for j in range(HEADS_PER_GROUP):
            head = g * HEADS_PER_GROUP + j
            sl = slice(head * SSM_HEAD_DIM, (head + 1) * SSM_HEAD_DIM)
            h0 = h_ref[0, sl, :]
            acol = acs8[:, head:head + 1]
            gmat = cb * jnp.exp(jnp.where(causal, acol - acst[head:head + 1, :], NEG_INF))
            xh = xa[:, sl]
            xdt = xh * dt8[:, head:head + 1]
            y = _dot_nt(cgb, h0.astype(BF16)) * jnp.exp(acol) + dsk[:, sl] * xh
            hn = h0 * cdec[:, head * n_new:head * n_new + 1]
            for s in range(n_new):
                y = y + gmat[:, s:s + 1] * xdt[s:s + 1, :]
                hn = hn + xw[:, head * n_new + s:head * n_new + s + 1] * bg[s:s + 1, :]
            y_ref[:, sl] = y
            hn_ref[0, sl, :] = hn
    y = y_ref[...] * _silu(z_ref[0])
    for g in range(SSM_GROUPS):
        sl = slice(g * GROUP_WIDTH, (g + 1) * GROUP_WIDTH)
        yg = y[:, sl]
        ms = jnp.mean(yg * yg, axis=-1, keepdims=True)
        o_ref[0, :, sl] = yg * lax.rsqrt(ms + EPS) * gs_ref[:, sl]


def ssd_sample(h0, xa8, z8, xt, rowpack, dt8, acs8, acst, dskx, gs, n_new):
    db, hd, n = h0.shape
    d_inner = SSM_GROUPS * GROUP_WIDTH
    nh = d_inner // SSM_HEAD_DIM
    per_b = lambda *shape: pl.BlockSpec((1,) + shape, lambda b: (b,) + (0,) * len(shape))
    vec = pl.BlockSpec((1, d_inner), lambda b: (0, 0))
    return pl.pallas_call(
        functools.partial(_ssd_sample_kernel, n_new=n_new),
        grid=(db,),
        in_specs=[per_b(hd, n), per_b(TOK8, xa8.shape[2]), per_b(TOK8, d_inner),
                  per_b(SSM_HEAD_DIM, nh * n_new), per_b(TOK8, nh * n_new),
                  per_b(TOK8, nh), per_b(TOK8, nh), per_b(nh, TOK8), vec, vec],
        out_specs=[per_b(TOK8, d_inner), per_b(hd, n)],
        out_shape=[jax.ShapeDtypeStruct((db, TOK8, d_inner), F32),
                   jax.ShapeDtypeStruct((db, hd, n), F32)],
        scratch_shapes=[pltpu.VMEM((TOK8, d_inner), F32)],
        compiler_params=_cparams("parallel"),
        name="ssd_sample",
    )(h0, xa8, z8, xt, rowpack, dt8, acs8, acst, dskx, gs)


def _attn_sample_kernel(pt_ref, q_ref, kn_ref, vn_ref, nb_ref, tm_ref, k_ref, v_ref, lf_ref,
                        o_ref, qt_ref, qb_ref, knp_ref, m_ref, l_ref, acc_ref, carry_ref, *, n_new):
    del pt_ref
    b = pl.program_id(0)
    p = pl.program_id(1)
    n_rows = n_new * N_HEADS

    @pl.when((b == 0) & (p == 0))
    def _():
        knp_ref[...] = jnp.zeros_like(knp_ref)

    @pl.when(p == 0)
    def _():
        qt_ref[...] = jnp.zeros_like(qt_ref)
        for t in range(n_new):
            for head in range(N_HEADS):
                r = t * N_HEADS + head
                cs = (head // KV_GROUP) * HEAD_DIM
                qt_ref[r:r + 1, cs:cs + HEAD_DIM] = q_ref[0, t:t + 1, head * HEAD_DIM:(head + 1) * HEAD_DIM]
        qb_ref[...] = qt_ref[...].astype(BF16)
        knp_ref[0:n_new, :] = kn_ref[0]
        m_ref[...] = jnp.full_like(m_ref, NEG_INF)
        l_ref[...] = jnp.zeros_like(l_ref)
        acc_ref[...] = jnp.zeros_like(acc_ref)
        carry_ref[...] = jnp.zeros_like(carry_ref)

    qb = qb_ref[...]
    s = _dot_nt(qb, k_ref[0, 0].astype(BF16))
    lft = lf_ref[0, 0]
    hi, mid, lo = _split3(lft)
    tm = tm_ref[...]
    r16 = _dot(hi, tm) + _dot(mid, tm) + _dot(lo, tm) + carry_ref[...]
    carry_ref[...] = carry_ref[...] + jnp.sum(lft, axis=-1, keepdims=True)
    s = s + jnp.concatenate([r16] * n_new, axis=0)
    m_old = m_ref[...]
    m_new = jnp.maximum(m_old, jnp.max(s, axis=-1, keepdims=True))
    pr = jnp.exp(s - m_new)
    alpha = jnp.exp(m_old - m_new)
    l_ref[...] = alpha * l_ref[...] + jnp.sum(pr, axis=-1, keepdims=True)
    acc_ref[...] = alpha * acc_ref[...] + _dot(pr.astype(BF16), v_ref[0, 0].astype(BF16))
    m_ref[...] = m_new

    @pl.when(p == pl.num_programs(1) - 1)
    def _():
        sn = _dot_nt(qb, knp_ref[...].astype(BF16)) + nb_ref[0]
        m_old = m_ref[...]
        m_new = jnp.maximum(m_old, jnp.max(sn, axis=-1, keepdims=True))
        pn = jnp.exp(sn - m_new)
        alpha = jnp.exp(m_old - m_new)
        l = alpha * l_ref[...] + jnp.sum(pn, axis=-1, keepdims=True)
        acc = alpha * acc_ref[...]
        vn = vn_ref[0]
        for j in range(n_new):
            acc = acc + pn[:, j:j + 1] * vn[j:j + 1, :]
        acc_ref[...] = acc / l
        for t in range(n_new):
            for head in range(N_HEADS):
                r = t * N_HEADS + head
                cs = (head // KV_GROUP) * HEAD_DIM
                o_ref[0, t:t + 1, head * HEAD_DIM:(head + 1) * HEAD_DIM] = acc_ref[r:r + 1, cs:cs + HEAD_DIM]


def attn_sample(layer, page_table, q, kn, vn, nb, cache_k, cache_v, cache_lft):
    db, n_new, _ = q.shape
    n_pages = page_table.shape[1]
    kvw = N_KV_HEADS * HEAD_DIM
    n_rows = n_new * N_HEADS
    idx = np.arange(PAGE)
    tm = jnp.asarray(idx[:, None] > idx[None, :], BF16)
    pt_flat = page_table.reshape(-1)
    page = lambda b, p, pt: (layer, pt[b * n_pages + (n_pages - 1 - p)], 0, 0)
    per_b = lambda *shape: pl.BlockSpec((1,) + shape, lambda b, p, pt: (b,) + (0,) * len(shape))
    grid_spec = pltpu.PrefetchScalarGridSpec(
        num_scalar_prefetch=1,
        grid=(db, n_pages),
        in_specs=[per_b(n_new, N_HEADS * HEAD_DIM), per_b(n_new, kvw), per_b(n_new, kvw),
                  per_b(n_rows, LANES),
                  pl.BlockSpec((PAGE, PAGE), lambda b, p, pt: (0, 0)),
                  pl.BlockSpec((1, 1, PAGE, kvw), page),
                  pl.BlockSpec((1, 1, PAGE, kvw), page),
                  pl.BlockSpec((1, 1, N_HEADS, PAGE), page)],
        out_specs=per_b(n_new, N_HEADS * HEAD_DIM),
        scratch_shapes=[pltpu.VMEM((n_rows, kvw), F32),
                        pltpu.VMEM((n_rows, kvw), BF16),
                        pltpu.VMEM((PAGE, kvw), F32),
                        pltpu.VMEM((n_rows, 1), F32),
                        pltpu.VMEM((n_rows, 1), F32),
                        pltpu.VMEM((n_rows, kvw), F32),
                        pltpu.VMEM((N_HEADS, 1), F32)],
    )
    return pl.pallas_call(
        functools.partial(_attn_sample_kernel, n_new=n_new),
        grid_spec=grid_spec,
        out_shape=jax.ShapeDtypeStruct((db, n_new, N_HEADS * HEAD_DIM), F32),
        compiler_params=_cparams("arbitrary", "arbitrary"),
        name="attn_sample",
    )(pt_flat, q, kn, vn, nb, tm, cache_k, cache_v, cache_lft)


def kernel(x_prompt, x_sample, cache_k, cache_v, cache_logf, state_ssm, state_conv, page_table,
           p_prompt, p_sample, g_ffn1, w_ffn1_in, w_ffn1_out, g_mix, w_in, b_f, g_q, g_k,
           conv_w, conv_b, dt_bias, a_log, d_skip, g_ssm, w_branch_attn, w_branch_ssm, w_out,
           g_ffn2, w_ffn2_in, w_ffn2_out, g_ple, w_ple_gate, w_ple_proj):
    depth = w_in.shape[0]
    bp, seq, d_model = x_prompt.shape
    db, n_new, _ = x_sample.shape
    mp, ms = bp * seq, db * n_new
    attn_w = N_HEADS * HEAD_DIM
    kv_w = N_KV_HEADS * HEAD_DIM
    d_inner = SSM_GROUPS * GROUP_WIDTH
    conv_dim = d_inner + 2 * SSM_GROUPS * D_STATE
    ssm_heads = d_inner // SSM_HEAD_DIM
    o_q, o_k, o_v = 0, attn_w, attn_w + kv_w
    o_f = o_v + kv_w
    o_z = o_f + N_HEADS
    o_x = o_z + d_inner
    o_dt = o_x + conv_dim
    o_ga = o_dt + ssm_heads
    o_gb = o_ga + d_model
    z_col, xbc_col = 0, d_inner
    ga_col = xbc_col + conv_dim
    gb_col = ga_col + d_model
    scale = HEAD_DIM ** -0.5

    n_pool = cache_k.shape[1]
    ck = cache_k.reshape(depth, n_pool, PAGE, kv_w)
    cv = cache_v.reshape(depth, n_pool, PAGE, kv_w)
    clft = jnp.swapaxes(cache_logf, 2, 3)

    h = jnp.concatenate([x_prompt.reshape(mp, d_model), x_sample.reshape(ms, d_model)], axis=0)
    outs = [[] for _ in range(10)]
    for i in range(depth):
        wl = w_in[i]
        w_q = wl[:, o_q:o_k].astype(BF16)
        w_k = wl[:, o_k:o_v].astype(BF16)
        w_v = wl[:, o_v:o_f].astype(BF16)
        w_wide = jnp.concatenate([wl[:, o_z:o_dt], wl[:, o_ga:]], axis=1).astype(BF16)
        w_small = jnp.concatenate(
            [wl[:, o_dt:o_ga], wl[:, o_f:o_z],
             jnp.zeros((d_model, LANES - ssm_heads - N_HEADS), F32)], axis=1).astype(BF16)
        gate_bias = jnp.concatenate(
            [dt_bias[i], b_f[i], jnp.zeros((LANES - ssm_heads - N_HEADS,), F32)]).reshape(1, LANES)
        alog_row = jnp.concatenate([a_log[i], jnp.zeros((LANES - ssm_heads,), F32)]).reshape(1, LANES)
        dskx = jnp.repeat(d_skip[i], SSM_HEAD_DIM).reshape(1, d_inner)
        gs = g_ssm[i].reshape(1, d_inner)
        cwi = conv_w[i]
        cbi = conv_b[i].reshape(1, conv_dim)

        act = mm_swiglu(rmsnorm(h, g_ffn1[i]), w_ffn1_in[i].astype(BF16))
        h = mm_residual(act, w_ffn1_out[i].astype(BF16), h, 0.5)

        u = rmsnorm(h, g_mix[i])
        (q_b,) = mm_headnorm(u, w_q, g_q[i], [BF16], mult=scale)
        k_f, k_b = mm_headnorm(u, w_k, g_k[i], [F32, BF16])
        v_f, v_b = mm_plain(u, w_v, [F32, BF16])
        (wide,) = mm_plain(u, w_wide, [F32])
        (small,) = mm_plain(u, w_small, [F32], tn=LANES)

        dt_p, lf_p, c_p, acs_p = gates(small[:mp], gate_bias, alog_row, seq, SSM_CHUNK)
        dt_s, lf_s, c_s, acs_s = gates(small[mp:], gate_bias, alog_row, n_new, n_new)
        fcols = slice(ssm_heads, ssm_heads + N_HEADS)

        ct = c_p[:, fcols].reshape(bp, seq, N_KV_HEADS, KV_GROUP).transpose(0, 2, 3, 1)
        ct = jnp.pad(ct, ((0, 0), (0, 0), (0, SUBLANES - KV_GROUP), (0, 0)))
        o_attn_p = attn_prompt(q_b, k_b, v_b, ct, bp, seq)
        pg = lambda a: a[:, :ssm_heads].reshape(bp, seq, SSM_GROUPS, HEADS_PER_GROUP).transpose(0, 2, 1, 3)
        acsg = pg(acs_p)
        o_ssm_p, st_p = ssd_prompt(wide, z_col, xbc_col, pg(dt_p), acsg, acsg.transpose(0, 1, 3, 2),
                                   cwi, cbi, dskx, gs, bp, seq)

        xbc_s = wide[mp:, xbc_col:ga_col].reshape(db, n_new, conv_dim)
        x8 = jnp.concatenate([jnp.zeros((db, TOK8 - n_new - (CONV_W - 1), conv_dim), F32),
                              state_conv[i], xbc_s], axis=1).reshape(db * TOK8, conv_dim)
        xact = conv_sample(x8, cwi, cbi).reshape(db, TOK8, conv_dim)[:, TOK8 - n_new:]
        tail = TOK8 - n_new
        xa8 = jnp.pad(xact, ((0, 0), (0, tail), (0, 0)))
        z8 = jnp.pad(wide[mp:, z_col:z_col + d_inner].reshape(db, n_new, d_inner), ((0, 0), (0, tail), (0, 0)))
        dts = dt_s[:, :ssm_heads].reshape(db, n_new, ssm_heads)
        acss = acs_s[:, :ssm_heads].reshape(db, n_new, ssm_heads)
        dt8 = jnp.pad(dts, ((0, 0), (0, tail), (0, 0)))
        acs8 = jnp.pad(acss, ((0, 0), (0, tail), (0, 0)), mode="edge")
        flat = lambda a: a.transpose(0, 2, 1).reshape(db, 1, ssm_heads * n_new)
        rowpack = jnp.concatenate(
            [flat(acss), jnp.repeat(acss[:, n_new - 1, :], n_new, axis=-1).reshape(db, 1, -1), flat(dts),
             jnp.zeros((db, TOK8 - 3, ssm_heads * n_new), F32)], axis=1)
        xt = xact[:, :, :d_inner].reshape(db, n_new, ssm_heads, SSM_HEAD_DIM).transpose(0, 3, 2, 1)
        xt = xt.reshape(db, SSM_HEAD_DIM, ssm_heads * n_new)
        o_ssm_s8, st_s = ssd_sample(state_ssm[i].reshape(db, d_inner, D_STATE), xa8, z8, xt, rowpack,
                                    dt8, acs8, acs8.transpose(0, 2, 1), dskx, gs, n_new)
        o_ssm_s = o_ssm_s8[:, :n_new].reshape(ms, d_inner).astype(BF16)

        cn = c_s[:, fcols].reshape(db, n_new, N_HEADS)
        tt = np.arange(n_new)
        nbias = jnp.where((tt[None, :] <= tt[:, None])[None, :, None, :],
                          -cn.transpose(0, 2, 1)[:, None, :, :], NEG_INF)
        nbias = jnp.pad(nbias.reshape(db, n_new * N_HEADS, n_new),
                        ((0, 0), (0, 0), (0, LANES - n_new)), constant_values=NEG_INF)
        o_attn_s = attn_sample(i, page_table,
                               q_b[mp:].astype(F32).reshape(db, n_new, attn_w),
                               k_b[mp:].astype(F32).reshape(db, n_new, kv_w),
                               v_f[mp:].reshape(db, n_new, kv_w),
                               nbias, ck, cv, clft)
        o_attn_s = o_attn_s.reshape(ms, attn_w).astype(BF16)

        o_attn = jnp.concatenate([o_attn_p, o_attn_s], axis=0)
        o_ssm = jnp.concatenate([o_ssm_p, o_ssm_s], axis=0)
        merged = mm_merge(o_attn, w_branch_attn[i].astype(BF16), o_ssm, w_branch_ssm[i].astype(BF16),
                          wide, ga_col, gb_col)
        h = mm_residual(merged, w_out[i].astype(BF16), h, 1.0)

        act = mm_swiglu(rmsnorm(h, g_ffn2[i]), w_ffn2_in[i].astype(BF16))
        h = mm_residual(act, w_ffn2_out[i].astype(BF16), h, 0.5)

        p_l = jnp.concatenate([p_prompt[i].reshape(mp, -1), p_sample[i].reshape(ms, -1)], axis=0).astype(BF16)
        h = mm_ple(rmsnorm(h, g_ple[i]), w_ple_gate[i].astype(BF16), p_l, w_ple_proj[i].astype(BF16), h)

        xbc_p = wide[:mp, xbc_col:ga_col].reshape(bp, seq, conv_dim)
        layer_out = (
            k_f[:mp].reshape(bp, seq, N_KV_HEADS, HEAD_DIM), v_f[:mp].reshape(bp, seq, N_KV_HEADS, HEAD_DIM),
            lf_p[:, fcols].reshape(bp, seq, N_HEADS), xbc_p[:, seq - (CONV_W - 1):],
            st_p.reshape(bp, ssm_heads, SSM_HEAD_DIM, D_STATE),
            k_f[mp:].reshape(db, n_new, N_KV_HEADS, HEAD_DIM), v_f[mp:].reshape(db, n_new, N_KV_HEADS, HEAD_DIM),
            lf_s[:, fcols].reshape(db, n_new, N_HEADS),
            jnp.concatenate([state_conv[i], xbc_s], axis=1)[:, n_new:n_new + CONV_W - 1],
            st_s.reshape(db, ssm_heads, SSM_HEAD_DIM, D_STATE))
        for lst, val in zip(outs, layer_out):
            lst.append(val)

    return (h[:mp].reshape(bp, seq, d_model), h[mp:].reshape(db, n_new, d_model)) + tuple(
        jnp.stack(lst) for lst in outs)
```

```python
import functools

import numpy as np
import jax
import jax.numpy as jnp
from jax import lax
from jax.experimental import pallas as pl
from jax.experimental.pallas import tpu as pltpu

F32 = jnp.float32
BF16 = jnp.bfloat16
EPS = 1e-6
NEG_INF = float("-inf")

LANES = 128
SUBLANES = 8
VMEM_LIMIT_BYTES = 56 * 1024 * 1024

HEAD_DIM = 128
N_HEADS = 16
N_KV_HEADS = 8
KV_GROUP = N_HEADS // N_KV_HEADS
SSM_HEAD_DIM = 64
SSM_GROUPS = 8
HEADS_PER_GROUP = 8
GROUP_WIDTH = HEADS_PER_GROUP * SSM_HEAD_DIM
D_STATE = 128
CONV_W = 4
SSM_CHUNK = 128
PAGE = 128

TM = 512
TN = 512
TQ = 512
TOK16 = 16
PAGES_PER_STEP = 4


def _cparams(*sem):
    return pltpu.CompilerParams(dimension_semantics=sem, vmem_limit_bytes=VMEM_LIMIT_BYTES)


def _sigmoid(x):
    return 1.0 / (1.0 + jnp.exp(-x))


def _silu(x):
    return x * _sigmoid(x)


def _dot(a, b):
    return jnp.dot(a, b, preferred_element_type=F32)


def _dot_nt(a, b):
    return lax.dot_general(a, b, (((1,), (1,)), ((), ())), preferred_element_type=F32)


def _dot_tn(a, b):
    return lax.dot_general(a, b, (((0,), (0,)), ((), ())), preferred_element_type=F32)


def _split3(x):
    hi = x.astype(BF16)
    r1 = x - hi.astype(F32)
    mid = r1.astype(BF16)
    lo = (r1 - mid.astype(F32)).astype(BF16)
    return hi, mid, lo


def _rmsnorm_kernel(x_ref, g_ref, o_ref):
    x = x_ref[...]
    ms = jnp.mean(x * x, axis=-1, keepdims=True)
    o_ref[...] = (x * lax.rsqrt(ms + EPS) * g_ref[...]).astype(o_ref.dtype)


def rmsnorm(x, g):
    m, d = x.shape
    return pl.pallas_call(
        _rmsnorm_kernel,
        grid=(m // TM,),
        in_specs=[pl.BlockSpec((TM, d), lambda i: (i, 0)),
                  pl.BlockSpec((1, d), lambda i: (0, 0))],
        out_specs=pl.BlockSpec((TM, d), lambda i: (i, 0)),
        out_shape=jax.ShapeDtypeStruct((m, d), BF16),
        compiler_params=_cparams("parallel"),
        name="rmsnorm",
    )(x, g.reshape(1, d))


def _cast_weights(pairs):
    @pl.when(pl.program_id(1) == 0)
    def _():
        for w_ref, ws_ref in pairs:
            ws_ref[...] = w_ref[...].astype(BF16)


_MM_SEM = ("parallel", "arbitrary")


def _swiglu_kernel(a_ref, wa_ref, wb_ref, o_ref, wsa_ref, wsb_ref):
    _cast_weights([(wa_ref, wsa_ref), (wb_ref, wsb_ref)])
    a = a_ref[...]
    o_ref[...] = (_silu(_dot(a, wsa_ref[...])) * _dot(a, wsb_ref[...])).astype(o_ref.dtype)


def mm_swiglu(a, w):
    m, k = a.shape
    dff = w.shape[1] // 2
    nj = dff // TN
    return pl.pallas_call(
        _swiglu_kernel,
        grid=(nj, m // TM),
        in_specs=[pl.BlockSpec((TM, k), lambda j, i: (i, 0)),
                  pl.BlockSpec((k, TN), lambda j, i: (0, j)),
                  pl.BlockSpec((k, TN), lambda j, i: (0, j + nj))],
        out_specs=pl.BlockSpec((TM, TN), lambda j, i: (i, j)),
        out_shape=jax.ShapeDtypeStruct((m, dff), BF16),
        scratch_shapes=[pltpu.VMEM((k, TN), BF16), pltpu.VMEM((k, TN), BF16)],
        compiler_params=_cparams(*_MM_SEM),
        name="mm_swiglu",
    )(a, w, w)


def _residual_kernel(a_ref, w_ref, r_ref, o_ref, ws_ref, *, alpha):
    _cast_weights([(w_ref, ws_ref)])
    o_ref[...] = r_ref[...] + alpha * _dot(a_ref[...], ws_ref[...])


def mm_residual(a, w, res, alpha):
    m, k = a.shape
    n = w.shape[1]
    return pl.pallas_call(
        functools.partial(_residual_kernel, alpha=alpha),
        grid=(n // TN, m // TM),
        in_specs=[pl.BlockSpec((TM, k), lambda j, i: (i, 0)),
                  pl.BlockSpec((k, TN), lambda j, i: (0, j)),
                  pl.BlockSpec((TM, TN), lambda j, i: (i, j))],
        out_specs=pl.BlockSpec((TM, TN), lambda j, i: (i, j)),
        out_shape=jax.ShapeDtypeStruct((m, n), F32),
        scratch_shapes=[pltpu.VMEM((k, TN), BF16)],
        compiler_params=_cparams(*_MM_SEM),
        name="mm_residual",
    )(a, w, res)


def _plain_kernel(a_ref, w_ref, *refs):
    *o_refs, ws_ref = refs
    _cast_weights([(w_ref, ws_ref)])
    acc = _dot(a_ref[...], ws_ref[...])
    for o_ref in o_refs:
        o_ref[...] = acc.astype(o_ref.dtype)


def mm_plain(a, w, out_dtypes, tn, col0=0, n=None):
    m, k = a.shape
    n = w.shape[1] if n is None else n
    j0 = col0 // tn
    return pl.pallas_call(
        _plain_kernel,
        grid=(n // tn, m // TM),
        in_specs=[pl.BlockSpec((TM, k), lambda j, i: (i, 0)),
                  pl.BlockSpec((k, tn), lambda j, i: (0, j + j0))],
        out_specs=[pl.BlockSpec((TM, tn), lambda j, i: (i, j)) for _ in out_dtypes],
        out_shape=[jax.ShapeDtypeStruct((m, n), dt) for dt in out_dtypes],
        scratch_shapes=[pltpu.VMEM((k, tn), BF16)],
        compiler_params=_cparams(*_MM_SEM),
        name="mm_plain",
    )(a, w)


def _headnorm_kernel(a_ref, w_ref, g_ref, *refs, mult):
    *o_refs, ws_ref = refs
    _cast_weights([(w_ref, ws_ref)])
    acc = _dot(a_ref[...], ws_ref[...])
    g = g_ref[...]
    for hh in range(acc.shape[1] // HEAD_DIM):
        sl = slice(hh * HEAD_DIM, (hh + 1) * HEAD_DIM)
        blk = acc[:, sl]
        ms = jnp.mean(blk * blk, axis=-1, keepdims=True)
        y = blk * lax.rsqrt(ms + EPS) * g
        for o_ref in o_refs:
            o_ref[:, sl] = (y * mult).astype(o_ref.dtype) if o_ref.dtype == BF16 else y


def mm_headnorm(a, w, g, out_dtypes, col0, n, mult=1.0):
    m, k = a.shape
    j0 = col0 // TN
    return pl.pallas_call(
        functools.partial(_headnorm_kernel, mult=mult),
        grid=(n // TN, m // TM),
        in_specs=[pl.BlockSpec((TM, k), lambda j, i: (i, 0)),
                  pl.BlockSpec((k, TN), lambda j, i: (0, j + j0)),
                  pl.BlockSpec((1, HEAD_DIM), lambda j, i: (0, 0))],
        out_specs=[pl.BlockSpec((TM, TN), lambda j, i: (i, j)) for _ in out_dtypes],
        out_shape=[jax.ShapeDtypeStruct((m, n), dt) for dt in out_dtypes],
        scratch_shapes=[pltpu.VMEM((k, TN), BF16)],
        compiler_params=_cparams(*_MM_SEM),
        name="mm_headnorm",
    )(a, w, g.reshape(1, HEAD_DIM))


def _merge_kernel(a1_ref, w1_ref, a2_ref, w2_ref, ga_ref, gb_ref, o_ref, ws1_ref, ws2_ref):
    _cast_weights([(w1_ref, ws1_ref), (w2_ref, ws2_ref)])
    o_ref[...] = (_sigmoid(ga_ref[...]) * _dot(a1_ref[...], ws1_ref[...])
                  + _sigmoid(gb_ref[...]) * _dot(a2_ref[...], ws2_ref[...])).astype(o_ref.dtype)


def mm_merge(a1, w1, a2, w2, gates, ga_col, gb_col):
    m, k1 = a1.shape
    k2 = a2.shape[1]
    n = w1.shape[1]
    ja, jb = ga_col // TN, gb_col // TN
    return pl.pallas_call(
        _merge_kernel,
        grid=(n // TN, m // TM),
        in_specs=[pl.BlockSpec((TM, k1), lambda j, i: (i, 0)),
                  pl.BlockSpec((k1, TN), lambda j, i: (0, j)),
                  pl.BlockSpec((TM, k2), lambda j, i: (i, 0)),
                  pl.BlockSpec((k2, TN), lambda j, i: (0, j)),
                  pl.BlockSpec((TM, TN), lambda j, i: (i, j + ja)),
                  pl.BlockSpec((TM, TN), lambda j, i: (i, j + jb))],
        out_specs=pl.BlockSpec((TM, TN), lambda j, i: (i, j)),
        out_shape=jax.ShapeDtypeStruct((m, n), BF16),
        scratch_shapes=[pltpu.VMEM((k1, TN), BF16), pltpu.VMEM((k2, TN), BF16)],
        compiler_params=_cparams(*_MM_SEM),
        name="mm_merge",
    )(a1, w1, a2, w2, gates, gates)


def _ple_kernel(a_ref, wg_ref, p_ref, wp_ref, r_ref, o_ref, wsg_ref, wsp_ref):
    _cast_weights([(wg_ref, wsg_ref), (wp_ref, wsp_ref)])
    gate = _sigmoid(_dot(a_ref[...], wsg_ref[...]))
    o_ref[...] = r_ref[...] + gate * _dot(p_ref[...], wsp_ref[...])


def mm_ple(a, wg, p, wp, res):
    m, k = a.shape
    kp = p.shape[1]
    n = wg.shape[1]
    return pl.pallas_call(
        _ple_kernel,
        grid=(n // TN, m // TM),
        in_specs=[pl.BlockSpec((TM, k), lambda j, i: (i, 0)),
                  pl.BlockSpec((k, TN), lambda j, i: (0, j)),
                  pl.BlockSpec((TM, kp), lambda j, i: (i, 0)),
                  pl.BlockSpec((kp, TN), lambda j, i: (0, j)),
                  pl.BlockSpec((TM, TN), lambda j, i: (i, j))],
        out_specs=pl.BlockSpec((TM, TN), lambda j, i: (i, j)),
        out_shape=jax.ShapeDtypeStruct((m, n), F32),
        scratch_shapes=[pltpu.VMEM((k, TN), BF16), pltpu.VMEM((kp, TN), BF16)],
        compiler_params=_cparams(*_MM_SEM),
        name="mm_ple",
    )(a, wg, p, wp, res)


def _tri_cumsum(t, x):
    hi, mid, lo = _split3(x)
    return _dot(t, hi) + _dot(t, mid) + _dot(t, lo)


def _gates_kernel(x_ref, bias_ref, alog_ref, tseq_ref, tchk_ref,
                  dt_ref, lf_ref, c_ref, acs_ref, carry_ref, *, blocks_per_seq):
    i = pl.program_id(0)
    xb = x_ref[...] + bias_ref[...]
    t = jnp.log1p(jnp.exp(-jnp.abs(xb)))
    sp = jnp.maximum(xb, 0.0) + t
    ls = jnp.minimum(xb, 0.0) - t
    dt_ref[...] = sp
    lf_ref[...] = ls
    a = sp * (-jnp.exp(alog_ref[...]))
    acs_ref[...] = _tri_cumsum(tchk_ref[...], a)

    @pl.when(i % blocks_per_seq == 0)
    def _():
        carry_ref[...] = jnp.zeros_like(carry_ref)

    c = _tri_cumsum(tseq_ref[...], ls) + carry_ref[...]
    c_ref[...] = c
    carry_ref[...] = c[TM - 1:TM, :]


def gates(x, bias, alog, seq_len, chunk):
    m = x.shape[0]
    idx = np.arange(TM)
    low = idx[:, None] >= idx[None, :]
    tseq = jnp.asarray(low & (idx[:, None] // seq_len == idx[None, :] // seq_len), BF16)
    tchk = jnp.asarray(low & (idx[:, None] // chunk == idx[None, :] // chunk), BF16)
    row = pl.BlockSpec((TM, LANES), lambda i: (i, 0))
    vec = pl.BlockSpec((1, LANES), lambda i: (0, 0))
    tri = pl.BlockSpec((TM, TM), lambda i: (0, 0))
    return pl.pallas_call(
        functools.partial(_gates_kernel, blocks_per_seq=max(1, seq_len // TM)),
        grid=(m // TM,),
        in_specs=[row, vec, vec, tri, tri],
        out_specs=[row, row, row, row],
        out_shape=[jax.ShapeDtypeStruct((m, LANES), F32)] * 4,
        scratch_shapes=[pltpu.VMEM((1, LANES), F32)],
        compiler_params=_cparams("arbitrary"),
        name="gates",
    )(x, bias, alog, tseq, tchk)


def _attn_prompt_kernel(q_ref, k_ref, v_ref, ct_ref, o_ref):
    qi = pl.program_id(2)
    q0 = pl.multiple_of(qi * TQ, TQ)
    row = lax.broadcasted_iota(jnp.int32, (TQ, TQ), 0)
    col = lax.broadcasted_iota(jnp.int32, (TQ, TQ), 1)
    causal = col <= row
    for g in range(KV_GROUP):
        q = q_ref[:, g * HEAD_DIM:(g + 1) * HEAD_DIM]
        cref = ct_ref[0, 0, g:g + 1, pl.ds(q0, LANES)][:, 0:1]

        def tile(ks, m, l, acc, masked):
            kb = k_ref[pl.ds(ks, TQ), :]
            vb = v_ref[pl.ds(ks, TQ), :]
            s = _dot_nt(q, kb) + (cref - ct_ref[0, 0, g:g + 1, pl.ds(ks, TQ)])
            if masked:
                s = jnp.where(causal, s, NEG_INF)
            m_new = jnp.maximum(m, jnp.max(s, axis=-1, keepdims=True))
            p = jnp.exp(s - m_new)
            alpha = jnp.exp(m - m_new)
            l = alpha * l + jnp.sum(p, axis=-1, keepdims=True)
            acc = alpha * acc + _dot(p.astype(BF16), vb)
            return m_new, l, acc

        def body(ki, carry):
            return tile(pl.multiple_of(ki * TQ, TQ), *carry, masked=False)

        init = (jnp.full((TQ, 1), NEG_INF, F32), jnp.zeros((TQ, 1), F32),
                jnp.zeros((TQ, HEAD_DIM), F32))
        m, l, acc = lax.fori_loop(0, qi, body, init)
        m, l, acc = tile(q0, m, l, acc, masked=True)
        o_ref[:, g * HEAD_DIM:(g + 1) * HEAD_DIM] = (acc / l).astype(o_ref.dtype)


def attn_prompt(q, k, v, ct, bsz, seq):
    nq = seq // TQ
    qw = KV_GROUP * HEAD_DIM
    return pl.pallas_call(
        _attn_prompt_kernel,
        grid=(bsz, N_KV_HEADS, nq),
        in_specs=[pl.BlockSpec((TQ, qw), lambda b, h, i: (b * nq + i, h)),
                  pl.BlockSpec((seq, HEAD_DIM), lambda b, h, i: (b, h)),
                  pl.BlockSpec((seq, HEAD_DIM), lambda b, h, i: (b, h)),
                  pl.BlockSpec((1, 1, SUBLANES, seq), lambda b, h, i: (b, h, 0, 0))],
        out_specs=pl.BlockSpec((TQ, qw), lambda b, h, i: (b * nq + i, h)),
        out_shape=jax.ShapeDtypeStruct((bsz * seq, N_HEADS * HEAD_DIM), BF16),
        compiler_params=_cparams("parallel", "parallel", "arbitrary"),
        name="attn_prompt",
    )(q, k, v, ct)


def _expand_heads(a):
    lane = lax.broadcasted_iota(jnp.int32, (a.shape[0], LANES), 1)
    blocks = [jnp.where(lane < SSM_HEAD_DIM, a[:, 2 * jj:2 * jj + 1], a[:, 2 * jj + 1:2 * jj + 2])
              for jj in range(HEADS_PER_GROUP // 2)]
    return jnp.concatenate(blocks, axis=1)


def _ssd_chunk(xs, bm, cm, z, dt8, acs8, acst, hprev, dsk, gs):
    q = xs.shape[0]
    dtx = _expand_heads(dt8)
    acsx = _expand_heads(acs8)
    acs_end = acsx[q - 1:q, :]
    xdt = xs * dtx
    cmb = cm.astype(BF16)
    bmb = bm.astype(BF16)
    cb = _dot_nt(cmb, bmb)
    row = lax.broadcasted_iota(jnp.int32, (q, q), 0)
    col = lax.broadcasted_iota(jnp.int32, (q, q), 1)
    causal = row >= col
    lane = lax.broadcasted_iota(jnp.int32, (q, LANES), 1)
    yblocks = []
    for jj in range(HEADS_PER_GROUP // 2):
        xblk = xdt[:, jj * LANES:(jj + 1) * LANES]
        acc = None
        for e in range(2):
            j = 2 * jj + e
            seg = acs8[:, j:j + 1] - acst[j:j + 1, :]
            lmat = jnp.exp(jnp.where(causal, seg, NEG_INF))
            mh = (cb * lmat).astype(BF16)
            keep = (lane < SSM_HEAD_DIM) if e == 0 else (lane >= SSM_HEAD_DIM)
            part = _dot(mh, jnp.where(keep, xblk, 0.0).astype(BF16))
            acc = part if acc is None else acc + part
        yblocks.append(acc)
    y_diag = jnp.concatenate(yblocks, axis=1)
    y_off = _dot_nt(cmb, hprev.astype(BF16)) * jnp.exp(acsx)
    xw = (xdt * jnp.exp(acs_end - acsx)).astype(BF16)
    states = _dot_tn(xw, bmb)
    hnew = []
    for j in range(HEADS_PER_GROUP):
        sl = slice(j * SSM_HEAD_DIM, (j + 1) * SSM_HEAD_DIM)
        hnew.append(hprev[sl, :] * jnp.exp(acst[j:j + 1, q - 1:q]) + states[sl, :])
    y = y_diag + y_off + dsk * xs
    y = y * _silu(z)
    ms = jnp.mean(y * y, axis=-1, keepdims=True)
    return y * lax.rsqrt(ms + EPS) * gs, hnew


def _conv_silu(cur_ref, prev_ref, w_ref, b_ref):
    cur = cur_ref[...]
    full = jnp.concatenate([prev_ref[...], cur], axis=0)
    rows = cur.shape[0]
    out = b_ref[...]
    for j in range(CONV_W):
        lo = SUBLANES - (CONV_W - 1) + j
        out = out + full[lo:lo + rows] * w_ref[j:j + 1, :]
    prev_ref[...] = cur[rows - SUBLANES:rows]
    return _silu(out)


def _ssd_prompt_kernel(x_ref, b_ref, c_ref, z_ref, dt_ref, acs_ref, acst_ref,
                       wx_ref, wb_ref, wc_ref, bx_ref, bb_ref, bc_ref, dsk_ref, gs_ref,
                       o_ref, st_ref, h_ref, px_ref, pb_ref, pc_ref):
    ci = pl.program_id(2)

    @pl.when(ci == 0)
    def _():
        h_ref[...] = jnp.zeros_like(h_ref)
        px_ref[...] = jnp.zeros_like(px_ref)
        pb_ref[...] = jnp.zeros_like(pb_ref)
        pc_ref[...] = jnp.zeros_like(pc_ref)

    xs = _conv_silu(x_ref, px_ref, wx_ref, bx_ref)
    bm = _conv_silu(b_ref, pb_ref, wb_ref, bb_ref)
    cm = _conv_silu(c_ref, pc_ref, wc_ref, bc_ref)
    y, hnew = _ssd_chunk(xs, bm, cm, z_ref[...], dt_ref[0, 0], acs_ref[0, 0], acst_ref[0, 0],
                         h_ref[...], dsk_ref[...], gs_ref[...])
    for j, hj in enumerate(hnew):
        h_ref[j * SSM_HEAD_DIM:(j + 1) * SSM_HEAD_DIM, :] = hj
    o_ref[...] = y.astype(o_ref.dtype)

    @pl.when(ci == pl.num_programs(2) - 1)
    def _():
        st_ref[0] = h_ref[...]


def ssd_prompt(zx, z_col, xbc_col, dtg, acsg, acstg, conv_w, conv_b, dskx, gs, bsz, seq):
    nc = seq // SSM_CHUNK
    d_inner = SSM_GROUPS * GROUP_WIDTH
    jz = z_col // GROUP_WIDTH
    jx = xbc_col // GROUP_WIDTH
    jb = (xbc_col + d_inner) // D_STATE
    jc = jb + SSM_GROUPS
    wjb = d_inner // D_STATE
    wjc = wjb + SSM_GROUPS
    q = SSM_CHUNK
    rowmap = lambda off: (lambda b, g, c: (b * nc + c, off + g))
    wmap = lambda off: (lambda b, g, c: (0, off + g))
    small = pl.BlockSpec((1, 1, q, HEADS_PER_GROUP), lambda b, g, c: (b, g, c, 0))
    in_specs = [
        pl.BlockSpec((q, GROUP_WIDTH), rowmap(jx)),
        pl.BlockSpec((q, D_STATE), rowmap(jb)),
        pl.BlockSpec((q, D_STATE), rowmap(jc)),
        pl.BlockSpec((q, GROUP_WIDTH), rowmap(jz)),
        small, small,
        pl.BlockSpec((1, 1, HEADS_PER_GROUP, q), lambda b, g, c: (b, g, 0, c)),
        pl.BlockSpec((CONV_W, GROUP_WIDTH), wmap(0)),
        pl.BlockSpec((CONV_W, D_STATE), wmap(wjb)),
        pl.BlockSpec((CONV_W, D_STATE), wmap(wjc)),
        pl.BlockSpec((1, GROUP_WIDTH), wmap(0)),
        pl.BlockSpec((1, D_STATE), wmap(wjb)),
        pl.BlockSpec((1, D_STATE), wmap(wjc)),
        pl.BlockSpec((1, GROUP_WIDTH), wmap(0)),
        pl.BlockSpec((1, GROUP_WIDTH), wmap(0)),
    ]
    return pl.pallas_call(
        _ssd_prompt_kernel,
        grid=(bsz, SSM_GROUPS, nc),
        in_specs=in_specs,
        out_specs=[pl.BlockSpec((q, GROUP_WIDTH), lambda b, g, c: (b * nc + c, g)),
                   pl.BlockSpec((1, GROUP_WIDTH, D_STATE), lambda b, g, c: (b, g, 0))],
        out_shape=[jax.ShapeDtypeStruct((bsz * seq, d_inner), BF16),
                   jax.ShapeDtypeStruct((bsz, d_inner, D_STATE), F32)],
        scratch_shapes=[pltpu.VMEM((GROUP_WIDTH, D_STATE), F32),
                        pltpu.VMEM((SUBLANES, GROUP_WIDTH), F32),
                        pltpu.VMEM((SUBLANES, D_STATE), F32),
                        pltpu.VMEM((SUBLANES, D_STATE), F32)],
        compiler_params=_cparams("parallel", "parallel", "arbitrary"),
        name="ssd_prompt",
    )(zx, zx, zx, zx, dtg, acsg, acstg, conv_w, conv_w, conv_w, conv_b, conv_b, conv_b, dskx, gs)


def _ssd_sample_kernel(h_ref, xbc_ref, hist_ref, z_ref, dt_ref, acs_ref, acst_ref,
                       wc_ref, bc_ref, dsk_ref, gs_ref, *refs, n_new, aliased):
    if aliased:
        refs = refs[1:]
    o_ref, hn_ref, xp_ref, xa_ref, z16_ref = refs
    d_inner = SSM_GROUPS * GROUP_WIDTH
    xp_ref[...] = jnp.zeros_like(xp_ref)
    xp_ref[0:CONV_W - 1, :] = hist_ref[0]
    xp_ref[CONV_W - 1:CONV_W - 1 + n_new, :] = xbc_ref[0]
    xp = xp_ref[...]
    out = bc_ref[...]
    for j in range(CONV_W):
        out = out + xp[j:j + n_new] * wc_ref[j:j + 1, :]
    xa_ref[...] = jnp.zeros_like(xa_ref)
    xa_ref[0:n_new, :] = _silu(out)
    z16_ref[...] = jnp.zeros_like(z16_ref)
    z16_ref[0:n_new, :] = z_ref[0]
    dt16 = dt_ref[0]
    acs16 = acs_ref[0]
    acst = acst_ref[0]
    for g in range(SSM_GROUPS):
        cs = slice(g * GROUP_WIDTH, (g + 1) * GROUP_WIDTH)
        hs = slice(g * HEADS_PER_GROUP, (g + 1) * HEADS_PER_GROUP)
        bcol = d_inner + g * D_STATE
        ccol = d_inner + (SSM_GROUPS + g) * D_STATE
        y, hnew = _ssd_chunk(xa_ref[:, cs], xa_ref[:, bcol:bcol + D_STATE], xa_ref[:, ccol:ccol + D_STATE],
                             z16_ref[:, cs], dt16[:, hs], acs16[:, hs], acst[hs, :],
                             h_ref[0, cs, :], dsk_ref[:, cs], gs_ref[:, cs])
        for j, hj in enumerate(hnew):
            r0 = g * GROUP_WIDTH + j * SSM_HEAD_DIM
            hn_ref[0, 0, r0:r0 + SSM_HEAD_DIM, :] = hj
        o_ref[0, :, cs] = y


def ssd_sample(layer, depth, h0, xbc, hist, z, dt16, acs16, acst, conv_w, conv_b, dskx, gs, prev_states):
    db, hd, n = h0.shape
    n_new = xbc.shape[1]
    conv_dim = xbc.shape[2]
    d_inner = SSM_GROUPS * GROUP_WIDTH
    nh = d_inner // SSM_HEAD_DIM
    per_b = lambda *shape: pl.BlockSpec((1,) + shape, lambda b: (b,) + (0,) * len(shape))
    full = lambda *shape: pl.BlockSpec(shape, lambda b: (0,) * len(shape))
    aliased = prev_states is not None
    in_specs = [per_b(hd, n), per_b(n_new, conv_dim), per_b(CONV_W - 1, conv_dim), per_b(n_new, d_inner),
                per_b(TOK16, nh), per_b(TOK16, nh), per_b(nh, TOK16),
                full(CONV_W, conv_dim), full(1, conv_dim), full(1, d_inner), full(1, d_inner)]
    args = [h0, xbc, hist, z, dt16, acs16, acst, conv_w, conv_b, dskx, gs]
    if aliased:
        in_specs.append(pl.BlockSpec(memory_space=pl.ANY))
        args.append(prev_states)
    return pl.pallas_call(
        functools.partial(_ssd_sample_kernel, n_new=n_new, aliased=aliased),
        grid=(db,),
        in_specs=in_specs,
        out_specs=[per_b(TOK16, d_inner),
                   pl.BlockSpec((1, 1, hd, n), lambda b: (layer, b, 0, 0))],
        out_shape=[jax.ShapeDtypeStruct((db, TOK16, d_inner), F32),
                   jax.ShapeDtypeStruct((depth, db, hd, n), F32)],
        scratch_shapes=[pltpu.VMEM((SUBLANES, conv_dim), F32),
                        pltpu.VMEM((TOK16, conv_dim), F32),
                        pltpu.VMEM((TOK16, d_inner), F32)],
        input_output_aliases={len(args) - 1: 1} if aliased else {},
        compiler_params=_cparams("parallel"),
        name="ssd_sample",
    )(*args)


def _page_head(page_ref, hk):
    return page_ref[0, 0, pl.ds(hk, PAGE, stride=N_KV_HEADS), :]


def _attn_sample_kernel(pt_ref, q_ref, kn_ref, vn_ref, nb_ref, tm_ref, *refs, n_new, pps):
    del pt_ref
    k_refs, v_refs, lf_refs = refs[0:pps], refs[pps:2 * pps], refs[2 * pps:3 * pps]
    o_ref, qs_ref, knp_ref, m_ref, l_ref, acc_ref, carry_ref, r_ref = refs[3 * pps:]
    b = pl.program_id(0)
    p = pl.program_id(1)
    rows = KV_GROUP * n_new

    @pl.when((b == 0) & (p == 0))
    def _():
        knp_ref[...] = jnp.zeros_like(knp_ref)

    @pl.when(p == 0)
    def _():
        for head in range(N_HEADS):
            qs_ref[head * n_new:(head + 1) * n_new, :] = q_ref[0, :, head * HEAD_DIM:(head + 1) * HEAD_DIM]
        knp_ref[0:n_new, :] = kn_ref[0]
        m_ref[...] = jnp.full_like(m_ref, NEG_INF)
        l_ref[...] = jnp.zeros_like(l_ref)
        acc_ref[...] = jnp.zeros_like(acc_ref)
        carry_ref[...] = jnp.zeros_like(carry_ref)

    tm = tm_ref[...]
    qh = [qs_ref[hk * rows:(hk + 1) * rows, :].astype(BF16) for hk in range(N_KV_HEADS)]
    s_cols = []
    for u in range(pps):
        lft = lf_refs[u][0, 0]
        hi, mid, lo = _split3(lft)
        r16 = _dot(hi, tm) + _dot(mid, tm) + _dot(lo, tm) + carry_ref[...]
        carry_ref[...] = carry_ref[...] + jnp.sum(lft, axis=-1, keepdims=True)
        for head in range(N_HEADS):
            r_ref[head * n_new:(head + 1) * n_new, u * PAGE:(u + 1) * PAGE] = jnp.broadcast_to(
                r16[head:head + 1, :], (n_new, PAGE))
        s_cols.append(jnp.concatenate(
            [_dot_nt(qh[hk], _page_head(k_refs[u], hk).astype(BF16)) for hk in range(N_KV_HEADS)], axis=0))
    s = jnp.concatenate(s_cols, axis=1) + r_ref[...]
    m_old = m_ref[...]
    m_new = jnp.maximum(m_old, jnp.max(s, axis=-1, keepdims=True))
    pr = jnp.exp(s - m_new)
    alpha = jnp.exp(m_old - m_new)
    l_ref[...] = alpha * l_ref[...] + jnp.sum(pr, axis=-1, keepdims=True)
    m_ref[...] = m_new
    acc_old = acc_ref[...]
    for hk in range(N_KV_HEADS):
        rs = slice(hk * rows, (hk + 1) * rows)
        upd = alpha[rs, :] * acc_old[rs, :]
        for u in range(pps):
            upd = upd + _dot(pr[rs, u * PAGE:(u + 1) * PAGE].astype(BF16),
                             _page_head(v_refs[u], hk).astype(BF16))
        acc_ref[rs, :] = upd

    @pl.when(p == pl.num_programs(1) - 1)
    def _():
        sn = jnp.concatenate(
            [_dot_nt(qh[hk], knp_ref[:, hk * HEAD_DIM:(hk + 1) * HEAD_DIM].astype(BF16))
             for hk in range(N_KV_HEADS)], axis=0) + nb_ref[0]
        m_old = m_ref[...]
        m_new = jnp.maximum(m_old, jnp.max(sn, axis=-1, keepdims=True))
        pn = jnp.exp(sn - m_new)
        alpha = jnp.exp(m_old - m_new)
        l = alpha * l_ref[...] + jnp.sum(pn, axis=-1, keepdims=True)
        acc = alpha * acc_ref[...]
        vn = vn_ref[0]
        for hk in range(N_KV_HEADS):
            rs = slice(hk * rows, (hk + 1) * rows)
            upd = acc[rs, :]
            for j in range(n_new):
                upd = upd + pn[rs, j:j + 1] * vn[j:j + 1, hk * HEAD_DIM:(hk + 1) * HEAD_DIM]
            acc_ref[rs, :] = upd / l[rs, :]
        for head in range(N_HEADS):
            o_ref[0, :, head * HEAD_DIM:(head + 1) * HEAD_DIM] = acc_ref[head * n_new:(head + 1) * n_new, :]


def attn_sample(layer, page_table, q, kn, vn, nb, cache_k, cache_v, cache_lft):
    db, n_new, _ = q.shape
    n_pages = page_table.shape[1]
    pps = PAGES_PER_STEP
    kvw = N_KV_HEADS * HEAD_DIM
    n_rows = n_new * N_HEADS
    idx = np.arange(PAGE)
    tm = jnp.asarray(idx[:, None] > idx[None, :], BF16)
    pt_flat = page_table.reshape(-1)

    def page(u, ndim):
        def index_map(b, p, pt):
            return (layer, pt[b * n_pages + (n_pages - 1 - (p * pps + u))]) + (0,) * ndim
        return index_map

    per_b = lambda *shape: pl.BlockSpec((1,) + shape, lambda b, p, pt: (b,) + (0,) * len(shape))
    kv_spec = lambda u: pl.BlockSpec((1, 1, PAGE * N_KV_HEADS, HEAD_DIM), page(u, 2))
    lf_spec = lambda u: pl.BlockSpec((1, 1, N_HEADS, PAGE), page(u, 2))
    grid_spec = pltpu.PrefetchScalarGridSpec(
        num_scalar_prefetch=1,
        grid=(db, n_pages // pps),
        in_specs=[per_b(n_new, N_HEADS * HEAD_DIM), per_b(n_new, kvw), per_b(n_new, kvw),
                  per_b(n_rows, LANES),
                  pl.BlockSpec((PAGE, PAGE), lambda b, p, pt: (0, 0))]
                 + [kv_spec(u) for u in range(pps)] + [kv_spec(u) for u in range(pps)]
                 + [lf_spec(u) for u in range(pps)],
        out_specs=per_b(n_new, N_HEADS * HEAD_DIM),
        scratch_shapes=[pltpu.VMEM((n_rows, HEAD_DIM), F32),
                        pltpu.VMEM((PAGE, kvw), F32),
                        pltpu.VMEM((n_rows, 1), F32),
                        pltpu.VMEM((n_rows, 1), F32),
                        pltpu.VMEM((n_rows, HEAD_DIM), F32),
                        pltpu.VMEM((N_HEADS, 1), F32),
                        pltpu.VMEM((n_rows, pps * PAGE), F32)],
    )
    return pl.pallas_call(
        functools.partial(_attn_sample_kernel, n_new=n_new, pps=pps),
        grid_spec=grid_spec,
        out_shape=jax.ShapeDtypeStruct((db, n_new, N_HEADS * HEAD_DIM), F32),
        compiler_params=_cparams("arbitrary", "arbitrary"),
        name="attn_sample",
    )(pt_flat, q, kn, vn, nb, tm, *([cache_k] * pps), *([cache_v] * pps), *([cache_lft] * pps))


def kernel(x_prompt, x_sample, cache_k, cache_v, cache_logf, state_ssm, state_conv, page_table,
           p_prompt, p_sample, g_ffn1, w_ffn1_in, w_ffn1_out, g_mix, w_in, b_f, g_q, g_k,
           conv_w, conv_b, dt_bias, a_log, d_skip, g_ssm, w_branch_attn, w_branch_ssm, w_out,
           g_ffn2, w_ffn2_in, w_ffn2_out, g_ple, w_ple_gate, w_ple_proj):
    depth = w_in.shape[0]
    bp, seq, d_model = x_prompt.shape
    db, n_new, _ = x_sample.shape
    mp, ms = bp * seq, db * n_new
    attn_w = N_HEADS * HEAD_DIM
    kv_w = N_KV_HEADS * HEAD_DIM
    d_inner = SSM_GROUPS * GROUP_WIDTH
    conv_dim = d_inner + 2 * SSM_GROUPS * D_STATE
    ssm_heads = d_inner // SSM_HEAD_DIM
    assert KV_GROUP * n_new == SUBLANES and page_table.shape[1] % PAGES_PER_STEP == 0
    o_q, o_k, o_v = 0, attn_w, attn_w + kv_w
    o_f = o_v + kv_w
    o_z = o_f + N_HEADS
    o_x = o_z + d_inner
    o_dt = o_x + conv_dim
    o_ga = o_dt + ssm_heads
    z_col, xbc_col = 0, d_inner
    ga_col = xbc_col + conv_dim
    gb_col = ga_col + d_model
    scale = HEAD_DIM ** -0.5

    n_pool = cache_k.shape[1]
    ck = cache_k.reshape(depth, n_pool, PAGE * N_KV_HEADS, HEAD_DIM)
    cv = cache_v.reshape(depth, n_pool, PAGE * N_KV_HEADS, HEAD_DIM)
    clft = jnp.swapaxes(cache_logf, 2, 3)

    h = jnp.concatenate([x_prompt.reshape(mp, d_model), x_sample.reshape(ms, d_model)], axis=0)
    outs = [[] for _ in range(9)]
    states_s = None
    for i in range(depth):
        wl = w_in[i]
        w_wide = jnp.concatenate([wl[:, o_z:o_dt], wl[:, o_ga:]], axis=1)
        w_small = jnp.concatenate(
            [wl[:, o_dt:o_ga], wl[:, o_f:o_z],
             jnp.zeros((d_model, LANES - ssm_heads - N_HEADS), F32)], axis=1)
        gate_bias = jnp.concatenate(
            [dt_bias[i], b_f[i], jnp.zeros((LANES - ssm_heads - N_HEADS,), F32)]).reshape(1, LANES)
        alog_row = jnp.concatenate([a_log[i], jnp.zeros((LANES - ssm_heads,), F32)]).reshape(1, LANES)
        dskx = jnp.repeat(d_skip[i], SSM_HEAD_DIM).reshape(1, d_inner)
        gs = g_ssm[i].reshape(1, d_inner)
        cwi = conv_w[i]
        cbi = conv_b[i].reshape(1, conv_dim)

        act = mm_swiglu(rmsnorm(h, g_ffn1[i]), w_ffn1_in[i])
        h = mm_residual(act, w_ffn1_out[i], h, 0.5)

        u = rmsnorm(h, g_mix[i])
        (q_b,) = mm_headnorm(u, wl, g_q[i], [BF16], o_q, attn_w, mult=scale)
        k_f, k_b = mm_headnorm(u, wl, g_k[i], [F32, BF16], o_k, kv_w)
        v_f, v_b = mm_plain(u, wl, [F32, BF16], TN, o_v, kv_w)
        (wide,) = mm_plain(u, w_wide, [F32], 2 * TN)
        (small,) = mm_plain(u, w_small, [F32], LANES)

        dt_p, lf_p, c_p, acs_p = gates(small[:mp], gate_bias, alog_row, seq, SSM_CHUNK)
        dt_s, lf_s, c_s, acs_s = gates(small[mp:], gate_bias, alog_row, n_new, n_new)
        fcols = slice(ssm_heads, ssm_heads + N_HEADS)

        ct = c_p[:, fcols].reshape(bp, seq, N_KV_HEADS, KV_GROUP).transpose(0, 2, 3, 1)
        ct = jnp.pad(ct, ((0, 0), (0, 0), (0, SUBLANES - KV_GROUP), (0, 0)))
        o_attn_p = attn_prompt(q_b, k_b, v_b, ct, bp, seq)
        pg = lambda a: a[:, :ssm_heads].reshape(bp, seq, SSM_GROUPS, HEADS_PER_GROUP).transpose(0, 2, 1, 3)
        acsg = pg(acs_p)
        o_ssm_p, st_p = ssd_prompt(wide, z_col, xbc_col, pg(dt_p), acsg, acsg.transpose(0, 1, 3, 2),
                                   cwi, cbi, dskx, gs, bp, seq)

        wide_s = wide[mp:].reshape(db, n_new, -1)
        xbc_s = wide_s[:, :, xbc_col:ga_col]
        tail = TOK16 - n_new
        dt16 = jnp.pad(dt_s[:, :ssm_heads].reshape(db, n_new, ssm_heads), ((0, 0), (0, tail), (0, 0)))
        acs16 = jnp.pad(acs_s[:, :ssm_heads].reshape(db, n_new, ssm_heads), ((0, 0), (0, tail), (0, 0)),
                        mode="edge")
        o_ssm_s16, states_s = ssd_sample(
            i, depth, state_ssm[i].reshape(db, d_inner, D_STATE), xbc_s, state_conv[i],
            wide_s[:, :, z_col:z_col + d_inner], dt16, acs16, acs16.transpose(0, 2, 1),
            cwi, cbi, dskx, gs, states_s)
        o_ssm_s = o_ssm_s16[:, :n_new].reshape(ms, d_inner).astype(BF16)

        cn = c_s[:, fcols].reshape(db, n_new, N_HEADS)
        tt = np.arange(n_new)
        nbias = jnp.where((tt[None, :] <= tt[:, None])[None, None, :, :],
                          -cn.transpose(0, 2, 1)[:, :, None, :], NEG_INF)
        nbias = jnp.pad(nbias.reshape(db, N_HEADS * n_new, n_new),
                        ((0, 0), (0, 0), (0, LANES - n_new)), constant_values=NEG_INF)
        o_attn_s = attn_sample(i, page_table,
                               q_b[mp:].astype(F32).reshape(db, n_new, attn_w),
                               k_b[mp:].astype(F32).reshape(db, n_new, kv_w),
                               v_f[mp:].reshape(db, n_new, kv_w),
                               nbias, ck, cv, clft)
        o_attn_s = o_attn_s.reshape(ms, attn_w).astype(BF16)

        o_attn = jnp.concatenate([o_attn_p, o_attn_s], axis=0)
        o_ssm = jnp.concatenate([o_ssm_p, o_ssm_s], axis=0)
        merged = mm_merge(o_attn, w_branch_attn[i], o_ssm, w_branch_ssm[i], wide, ga_col, gb_col)
        h = mm_residual(merged, w_out[i], h, 1.0)

        act = mm_swiglu(rmsnorm(h, g_ffn2[i]), w_ffn2_in[i])
        h = mm_residual(act, w_ffn2_out[i], h, 0.5)

        p_l = jnp.concatenate([p_prompt[i].reshape(mp, -1), p_sample[i].reshape(ms, -1)], axis=0).astype(BF16)
        h = mm_ple(rmsnorm(h, g_ple[i]), w_ple_gate[i], p_l, w_ple_proj[i], h)

        xbc_p = wide[:mp, xbc_col:ga_col].reshape(bp, seq, conv_dim)
        layer_out = (
            k_f[:mp].reshape(bp, seq, N_KV_HEADS, HEAD_DIM), v_f[:mp].reshape(bp, seq, N_KV_HEADS, HEAD_DIM),
            lf_p[:, fcols].reshape(bp, seq, N_HEADS), xbc_p[:, seq - (CONV_W - 1):],
            st_p.reshape(bp, ssm_heads, SSM_HEAD_DIM, D_STATE),
            k_f[mp:].reshape(db, n_new, N_KV_HEADS, HEAD_DIM), v_f[mp:].reshape(db, n_new, N_KV_HEADS, HEAD_DIM),
            lf_s[:, fcols].reshape(db, n_new, N_HEADS),
            jnp.concatenate([state_conv[i], xbc_s], axis=1)[:, n_new:n_new + CONV_W - 1])
        for lst, val in zip(outs, layer_out):
            lst.append(val)

    stacked = [jnp.stack(lst) for lst in outs]
    return (h[:mp].reshape(bp, seq, d_model), h[mp:].reshape(db, n_new, d_model), *stacked,
            states_s.reshape(depth, db, ssm_heads, SSM_HEAD_DIM, D_STATE))
```

```python
import functools

import numpy as np
import jax
import jax.numpy as jnp
from jax import lax
from jax.experimental import pallas as pl
from jax.experimental.pallas import tpu as pltpu

F32 = jnp.float32
BF16 = jnp.bfloat16
EPS = 1e-6
NEG_INF = float("-inf")

LANES = 128
SUBLANES = 8
VMEM_LIMIT_BYTES = 56 * 1024 * 1024

HEAD_DIM = 128
N_HEADS = 16
N_KV_HEADS = 8
KV_GROUP = N_HEADS // N_KV_HEADS
SSM_HEAD_DIM = 64
SSM_GROUPS = 8
HEADS_PER_GROUP = 8
GROUP_WIDTH = HEADS_PER_GROUP * SSM_HEAD_DIM
D_STATE = 128
CONV_W = 4
SSM_CHUNK = 128
PAGE = 128

TM = 512
TM_WIDE = 1088
TN = 512
TQ = 512
RB = 512
LOG2E = 1.4426950408889634
TOK16 = 16
PAGES_PER_STEP = 8


def _cparams(*sem):
    return pltpu.CompilerParams(dimension_semantics=sem, vmem_limit_bytes=VMEM_LIMIT_BYTES)


def _sigmoid(x):
    return 1.0 / (1.0 + jnp.exp(-x))


def _silu(x):
    return x * _sigmoid(x)


def _dot(a, b):
    return jnp.dot(a, b, preferred_element_type=F32)


def _dot_nt(a, b):
    return lax.dot_general(a, b, (((1,), (1,)), ((), ())), preferred_element_type=F32)


def _dot_tn(a, b):
    return lax.dot_general(a, b, (((0,), (0,)), ((), ())), preferred_element_type=F32)


def _split3(x):
    hi = x.astype(BF16)
    r1 = x - hi.astype(F32)
    mid = r1.astype(BF16)
    lo = (r1 - mid.astype(F32)).astype(BF16)
    return hi, mid, lo


def _rmsnorm_kernel(x_ref, g_ref, o_ref):
    x = x_ref[...]
    ms = jnp.mean(x * x, axis=-1, keepdims=True)
    o_ref[...] = (x * lax.rsqrt(ms + EPS) * g_ref[...]).astype(o_ref.dtype)


def rmsnorm(x, g):
    m, d = x.shape
    return pl.pallas_call(
        _rmsnorm_kernel,
        grid=(m // TM,),
        in_specs=[pl.BlockSpec((TM, d), lambda i: (i, 0)),
                  pl.BlockSpec((1, d), lambda i: (0, 0))],
        out_specs=pl.BlockSpec((TM, d), lambda i: (i, 0)),
        out_shape=jax.ShapeDtypeStruct((m, d), BF16),
        compiler_params=_cparams("parallel"),
        name="rmsnorm",
    )(x, g.reshape(1, d))


def _cast_weights(pairs):
    @pl.when(pl.program_id(1) == 0)
    def _():
        for w_ref, ws_ref in pairs:
            ws_ref[...] = w_ref[...].astype(BF16)


_MM_SEM = ("parallel", "arbitrary")


def _wspec(w, layer, k, tn, j0=0):
    if w.ndim == 2:
        return pl.BlockSpec((k, tn), lambda j, i: (0, j + j0))
    return pl.BlockSpec((None, k, tn), lambda j, i: (layer, 0, j + j0))


def _swiglu_kernel(a_ref, wa_ref, wb_ref, o_ref, wsa_ref, wsb_ref):
    _cast_weights([(wa_ref, wsa_ref), (wb_ref, wsb_ref)])
    a = a_ref[...]
    o_ref[...] = (_silu(_dot(a, wsa_ref[...])) * _dot(a, wsb_ref[...])).astype(o_ref.dtype)


def mm_swiglu(a, w, layer):
    m, k = a.shape
    dff = w.shape[-1] // 2
    nj = dff // TN
    tm = TM_WIDE
    return pl.pallas_call(
        _swiglu_kernel,
        grid=(nj, m // tm),
        in_specs=[pl.BlockSpec((tm, k), lambda j, i: (i, 0)),
                  _wspec(w, layer, k, TN), _wspec(w, layer, k, TN, nj)],
        out_specs=pl.BlockSpec((tm, TN), lambda j, i: (i, j)),
        out_shape=jax.ShapeDtypeStruct((m, dff), BF16),
        scratch_shapes=[pltpu.VMEM((k, TN), BF16), pltpu.VMEM((k, TN), BF16)],
        compiler_params=_cparams(*_MM_SEM),
        name="mm_swiglu",
    )(a, w, w)


def _residual_kernel(a_ref, w_ref, r_ref, o_ref, ws_ref, *, alpha):
    _cast_weights([(w_ref, ws_ref)])
    o_ref[...] = r_ref[...] + alpha * _dot(a_ref[...], ws_ref[...])


def mm_residual(a, w, layer, res, alpha, tm):
    m, k = a.shape
    n = w.shape[-1]
    return pl.pallas_call(
        functools.partial(_residual_kernel, alpha=alpha),
        grid=(n // TN, m // tm),
        in_specs=[pl.BlockSpec((tm, k), lambda j, i: (i, 0)),
                  _wspec(w, layer, k, TN),
                  pl.BlockSpec((tm, TN), lambda j, i: (i, j))],
        out_specs=pl.BlockSpec((tm, TN), lambda j, i: (i, j)),
        out_shape=jax.ShapeDtypeStruct((m, n), F32),
        scratch_shapes=[pltpu.VMEM((k, TN), BF16)],
        compiler_params=_cparams(*_MM_SEM),
        name="mm_residual",
    )(a, w, res)


def _plain_kernel(a_ref, w_ref, *refs):
    *o_refs, ws_ref = refs
    _cast_weights([(w_ref, ws_ref)])
    acc = _dot(a_ref[...], ws_ref[...])
    for o_ref in o_refs:
        o_ref[...] = acc.astype(o_ref.dtype)


def mm_plain(a, w, layer, out_dtypes, tn, col0=0, n=None):
    m, k = a.shape
    n = w.shape[-1] if n is None else n
    j0 = col0 // tn
    tm = TM_WIDE
    return pl.pallas_call(
        _plain_kernel,
        grid=(n // tn, m // tm),
        in_specs=[pl.BlockSpec((tm, k), lambda j, i: (i, 0)),
                  _wspec(w, layer, k, tn, j0)],
        out_specs=[pl.BlockSpec((tm, tn), lambda j, i: (i, j)) for _ in out_dtypes],
        out_shape=[jax.ShapeDtypeStruct((m, n), dt) for dt in out_dtypes],
        scratch_shapes=[pltpu.VMEM((k, tn), BF16)],
        compiler_params=_cparams(*_MM_SEM),
        name="mm_plain",
    )(a, w)


def _headnorm_kernel(a_ref, w_ref, g_ref, *refs, mult):
    *o_refs, ws_ref = refs
    _cast_weights([(w_ref, ws_ref)])
    acc = _dot(a_ref[...], ws_ref[...])
    g = g_ref[...]
    for hh in range(acc.shape[1] // HEAD_DIM):
        sl = slice(hh * HEAD_DIM, (hh + 1) * HEAD_DIM)
        blk = acc[:, sl]
        ms = jnp.mean(blk * blk, axis=-1, keepdims=True)
        y = blk * lax.rsqrt(ms + EPS) * g
        for o_ref in o_refs:
            o_ref[:, sl] = (y * mult).astype(o_ref.dtype) if o_ref.dtype == BF16 else y


def mm_headnorm(a, w, layer, g, out_dtypes, col0, n, mult=1.0):
    m, k = a.shape
    j0 = col0 // TN
    tm = TM_WIDE
    return pl.pallas_call(
        functools.partial(_headnorm_kernel, mult=mult),
        grid=(n // TN, m // tm),
        in_specs=[pl.BlockSpec((tm, k), lambda j, i: (i, 0)),
                  _wspec(w, layer, k, TN, j0),
                  pl.BlockSpec((1, HEAD_DIM), lambda j, i: (0, 0))],
        out_specs=[pl.BlockSpec((tm, TN), lambda j, i: (i, j)) for _ in out_dtypes],
        out_shape=[jax.ShapeDtypeStruct((m, n), dt) for dt in out_dtypes],
        scratch_shapes=[pltpu.VMEM((k, TN), BF16)],
        compiler_params=_cparams(*_MM_SEM),
        name="mm_headnorm",
    )(a, w, g.reshape(1, HEAD_DIM))


def _merge_kernel(a1_ref, w1_ref, a2_ref, w2_ref, ga_ref, gb_ref, o_ref, ws1_ref, ws2_ref):
    _cast_weights([(w1_ref, ws1_ref), (w2_ref, ws2_ref)])
    o_ref[...] = (_sigmoid(ga_ref[...]) * _dot(a1_ref[...], ws1_ref[...])
                  + _sigmoid(gb_ref[...]) * _dot(a2_ref[...], ws2_ref[...])).astype(o_ref.dtype)


def mm_merge(a1, w1, a2, w2, layer, gates, ga_col, gb_col):
    m, k1 = a1.shape
    k2 = a2.shape[1]
    n = w1.shape[-1]
    ja, jb = ga_col // TN, gb_col // TN
    return pl.pallas_call(
        _merge_kernel,
        grid=(n // TN, m // TM),
        in_specs=[pl.BlockSpec((TM, k1), lambda j, i: (i, 0)),
                  _wspec(w1, layer, k1, TN),
                  pl.BlockSpec((TM, k2), lambda j, i: (i, 0)),
                  _wspec(w2, layer, k2, TN),
                  pl.BlockSpec((TM, TN), lambda j, i: (i, j + ja)),
                  pl.BlockSpec((TM, TN), lambda j, i: (i, j + jb))],
        out_specs=pl.BlockSpec((TM, TN), lambda j, i: (i, j)),
        out_shape=jax.ShapeDtypeStruct((m, n), BF16),
        scratch_shapes=[pltpu.VMEM((k1, TN), BF16), pltpu.VMEM((k2, TN), BF16)],
        compiler_params=_cparams(*_MM_SEM),
        name="mm_merge",
    )(a1, w1, a2, w2, gates, gates)


def _ple_kernel(a_ref, wg_ref, p_ref, wp_ref, r_ref, o_ref, wsg_ref, wsp_ref):
    _cast_weights([(wg_ref, wsg_ref), (wp_ref, wsp_ref)])
    gate = _sigmoid(_dot(a_ref[...], wsg_ref[...]))
    o_ref[...] = r_ref[...] + gate * _dot(p_ref[...], wsp_ref[...])


def mm_ple(a, wg, p, wp, layer, res):
    m, k = a.shape
    kp = p.shape[1]
    n = wg.shape[-1]
    tm = TM_WIDE
    return pl.pallas_call(
        _ple_kernel,
        grid=(n // TN, m // tm),
        in_specs=[pl.BlockSpec((tm, k), lambda j, i: (i, 0)),
                  _wspec(wg, layer, k, TN),
                  pl.BlockSpec((tm, kp), lambda j, i: (i, 0)),
                  _wspec(wp, layer, kp, TN),
                  pl.BlockSpec((tm, TN), lambda j, i: (i, j))],
        out_specs=pl.BlockSpec((tm, TN), lambda j, i: (i, j)),
        out_shape=jax.ShapeDtypeStruct((m, n), F32),
        scratch_shapes=[pltpu.VMEM((k, TN), BF16), pltpu.VMEM((kp, TN), BF16)],
        compiler_params=_cparams(*_MM_SEM),
        name="mm_ple",
    )(a, wg, p, wp, res)


def _tri_cumsum(t, x):
    hi, mid, lo = _split3(x)
    return _dot(t, hi) + _dot(t, mid) + _dot(t, lo)


def _gates_kernel(x_ref, bias_ref, alog_ref, tseq_ref, tchk_ref,
                  dt_ref, lf_ref, c_ref, acs_ref, carry_ref, *, blocks_per_seq):
    i = pl.program_id(0)
    xb = x_ref[...] + bias_ref[...]
    t = jnp.log1p(jnp.exp(-jnp.abs(xb)))
    sp = jnp.maximum(xb, 0.0) + t
    ls = jnp.minimum(xb, 0.0) - t
    dt_ref[...] = sp
    lf_ref[...] = ls
    a = sp * (-jnp.exp(alog_ref[...]))
    acs_ref[...] = _tri_cumsum(tchk_ref[...], a)

    @pl.when(i % blocks_per_seq == 0)
    def _():
        carry_ref[...] = jnp.zeros_like(carry_ref)

    c = _tri_cumsum(tseq_ref[...], ls) + carry_ref[...]
    c_ref[...] = c
    carry_ref[...] = c[TM - 1:TM, :]


def gates(x, bias, alog, seq_len, chunk):
    m = x.shape[0]
    idx = np.arange(TM)
    low = idx[:, None] >= idx[None, :]
    tseq = jnp.asarray(low & (idx[:, None] // seq_len == idx[None, :] // seq_len), BF16)
    tchk = jnp.asarray(low & (idx[:, None] // chunk == idx[None, :] // chunk), BF16)
    row = pl.BlockSpec((TM, LANES), lambda i: (i, 0))
    vec = pl.BlockSpec((1, LANES), lambda i: (0, 0))
    tri = pl.BlockSpec((TM, TM), lambda i: (0, 0))
    return pl.pallas_call(
        functools.partial(_gates_kernel, blocks_per_seq=max(1, seq_len // TM)),
        grid=(m // TM,),
        in_specs=[row, vec, vec, tri, tri],
        out_specs=[row, row, row, row],
        out_shape=[jax.ShapeDtypeStruct((m, LANES), F32)] * 4,
        scratch_shapes=[pltpu.VMEM((1, LANES), F32)],
        compiler_params=_cparams("arbitrary"),
        name="gates",
    )(x, bias, alog, tseq, tchk)


def _attn_prompt_kernel(q_ref, k_ref, v_ref, ct_ref, o_ref):
    qi = pl.program_id(2)
    q0 = pl.multiple_of(qi * TQ, TQ)
    chains = [(g, rb) for g in range(KV_GROUP) for rb in range(TQ // RB)]
    crefs = [ct_ref[0, 0, g:g + 1, pl.ds(q0, LANES)][:, 0:1] for g in range(KV_GROUP)]

    def tile(g, rb, ks, nk, m, l, acc, masked):
        q = q_ref[rb * RB:(rb + 1) * RB, g * HEAD_DIM:(g + 1) * HEAD_DIM]
        kb = k_ref[pl.ds(ks, nk), :]
        vb = v_ref[pl.ds(ks, nk), :]
        s = _dot_nt(q, kb) + (crefs[g] - ct_ref[0, 0, g:g + 1, pl.ds(ks, nk)]) * LOG2E
        if masked:
            row = lax.broadcasted_iota(jnp.int32, (RB, nk), 0) + rb * RB
            col = lax.broadcasted_iota(jnp.int32, (RB, nk), 1)
            s = jnp.where(col <= row, s, NEG_INF)
        m_new = jnp.maximum(m, jnp.max(s, axis=-1, keepdims=True))
        p = jnp.exp2(s - m_new)
        alpha = jnp.exp2(m - m_new)
        l = alpha * l + jnp.sum(p, axis=-1, keepdims=True)
        acc = alpha * acc + _dot(p.astype(BF16), vb)
        return m_new, l, acc

    def body(ki, carry):
        ks = pl.multiple_of(ki * TQ, TQ)
        return tuple(tile(g, rb, ks, TQ, *carry[n], masked=False) for n, (g, rb) in enumerate(chains))

    init = tuple((jnp.full((RB, 1), NEG_INF, F32), jnp.zeros((RB, 1), F32), jnp.zeros((RB, HEAD_DIM), F32))
                 for _ in chains)
    carry = lax.fori_loop(0, qi, body, init)
    for n, (g, rb) in enumerate(chains):
        m, l, acc = tile(g, rb, q0, (rb + 1) * RB, *carry[n], masked=True)
        o_ref[rb * RB:(rb + 1) * RB, g * HEAD_DIM:(g + 1) * HEAD_DIM] = (acc / l).astype(o_ref.dtype)


def attn_prompt(q, k, v, ct, bsz, seq):
    nq = seq // TQ
    qw = KV_GROUP * HEAD_DIM
    return pl.pallas_call(
        _attn_prompt_kernel,
        grid=(bsz, N_KV_HEADS, nq),
        in_specs=[pl.BlockSpec((TQ, qw), lambda b, h, i: (b * nq + i, h)),
                  pl.BlockSpec((seq, HEAD_DIM), lambda b, h, i: (b, h)),
                  pl.BlockSpec((seq, HEAD_DIM), lambda b, h, i: (b, h)),
                  pl.BlockSpec((1, 1, SUBLANES, seq), lambda b, h, i: (b, h, 0, 0))],
        out_specs=pl.BlockSpec((TQ, qw), lambda b, h, i: (b * nq + i, h)),
        out_shape=jax.ShapeDtypeStruct((bsz * seq, N_HEADS * HEAD_DIM), BF16),
        compiler_params=_cparams("parallel", "parallel", "arbitrary"),
        name="attn_prompt",
    )(q, k, v, ct)


def _expand_heads(a):
    lane = lax.broadcasted_iota(jnp.int32, (a.shape[0], LANES), 1)
    blocks = [jnp.where(lane < SSM_HEAD_DIM, a[:, 2 * jj:2 * jj + 1], a[:, 2 * jj + 1:2 * jj + 2])
              for jj in range(HEADS_PER_GROUP // 2)]
    return jnp.concatenate(blocks, axis=1)


def _ssd_chunk(xs, bm, cm, z, dt8, acs8, acst, hprev, dsk, gs):
    q = xs.shape[0]
    dtx = _expand_heads(dt8)
    acsx = _expand_heads(acs8)
    acs_end = acsx[q - 1:q, :]
    xdt = xs * dtx
    cmb = cm.astype(BF16)
    bmb = bm.astype(BF16)
    cb = _dot_nt(cmb, bmb)
    row = lax.broadcasted_iota(jnp.int32, (q, q), 0)
    col = lax.broadcasted_iota(jnp.int32, (q, q), 1)
    causal = row >= col
    lane = lax.broadcasted_iota(jnp.int32, (q, LANES), 1)
    yblocks = []
    for jj in range(HEADS_PER_GROUP // 2):
        xblk = xdt[:, jj * LANES:(jj + 1) * LANES]
        acc = None
        for e in range(2):
            j = 2 * jj + e
            seg = acs8[:, j:j + 1] - acst[j:j + 1, :]
            lmat = jnp.exp(jnp.where(causal, seg, NEG_INF))
            mh = (cb * lmat).astype(BF16)
            keep = (lane < SSM_HEAD_DIM) if e == 0 else (lane >= SSM_HEAD_DIM)
            part = _dot(mh, jnp.where(keep, xblk, 0.0).astype(BF16))
            acc = part if acc is None else acc + part
        yblocks.append(acc)
    y_diag = jnp.concatenate(yblocks, axis=1)
    y_off = _dot_nt(cmb, hprev.astype(BF16)) * jnp.exp(acsx)
    xw = (xdt * jnp.exp(acs_end - acsx)).astype(BF16)
    states = _dot_tn(xw, bmb)
    hnew = []
    for j in range(HEADS_PER_GROUP):
        sl = slice(j * SSM_HEAD_DIM, (j + 1) * SSM_HEAD_DIM)
        hnew.append(hprev[sl, :] * jnp.exp(acst[j:j + 1, q - 1:q]) + states[sl, :])
    y = y_diag + y_off + dsk * xs
    y = y * _silu(z)
    ms = jnp.mean(y * y, axis=-1, keepdims=True)
    return y * lax.rsqrt(ms + EPS) * gs, hnew


def _conv_silu(cur_ref, buf_ref, w_ref, b_ref):
    rows = cur_ref.shape[0]
    buf_ref[SUBLANES:SUBLANES + rows, :] = cur_ref[...]
    out = b_ref[...]
    for j in range(CONV_W):
        lo = SUBLANES - (CONV_W - 1) + j
        out = out + buf_ref[lo:lo + rows, :] * w_ref[j:j + 1, :]
    buf_ref[0:SUBLANES, :] = buf_ref[rows:rows + SUBLANES, :]
    return _silu(out)


def _ssd_prompt_kernel(x_ref, b_ref, c_ref, z_ref, dt_ref, acs_ref, acst_ref,
                       wx_ref, wb_ref, wc_ref, bx_ref, bb_ref, bc_ref, dsk_ref, gs_ref,
                       o_ref, st_ref, h_ref, px_ref, pb_ref, pc_ref):
    ci = pl.program_id(2)

    @pl.when(ci == 0)
    def _():
        h_ref[...] = jnp.zeros_like(h_ref)
        for buf_ref in (px_ref, pb_ref, pc_ref):
            buf_ref[0:SUBLANES, :] = jnp.zeros((SUBLANES, buf_ref.shape[1]), F32)

    xs = _conv_silu(x_ref, px_ref, wx_ref, bx_ref)
    bm = _conv_silu(b_ref, pb_ref, wb_ref, bb_ref)
    cm = _conv_silu(c_ref, pc_ref, wc_ref, bc_ref)
    y, hnew = _ssd_chunk(xs, bm, cm, z_ref[...], dt_ref[0, 0], acs_ref[0, 0], acst_ref[0, 0],
                         h_ref[...], dsk_ref[...], gs_ref[...])
    for j, hj in enumerate(hnew):
        h_ref[j * SSM_HEAD_DIM:(j + 1) * SSM_HEAD_DIM, :] = hj
    o_ref[...] = y.astype(o_ref.dtype)

    @pl.when(ci == pl.num_programs(2) - 1)
    def _():
        st_ref[0] = h_ref[...]


def ssd_prompt(layer, zx, z_col, xbc_col, dtg, acsg, acstg, conv_w, conv_b, dskx, gs, bsz, seq):
    nc = seq // SSM_CHUNK
    d_inner = SSM_GROUPS * GROUP_WIDTH
    jz = z_col // GROUP_WIDTH
    jx = xbc_col // GROUP_WIDTH
    jb = (xbc_col + d_inner) // D_STATE
    jc = jb + SSM_GROUPS
    wjb = d_inner // D_STATE
    wjc = wjb + SSM_GROUPS
    q = SSM_CHUNK
    rowmap = lambda off: (lambda b, g, c: (b * nc + c, off + g))
    wmap = lambda off: (lambda b, g, c: (0, off + g))
    lmap = lambda off: (lambda b, g, c: (layer, 0, off + g))
    small = pl.BlockSpec((1, 1, q, HEADS_PER_GROUP), lambda b, g, c: (b, g, c, 0))
    in_specs = [
        pl.BlockSpec((q, GROUP_WIDTH), rowmap(jx)),
        pl.BlockSpec((q, D_STATE), rowmap(jb)),
        pl.BlockSpec((q, D_STATE), rowmap(jc)),
        pl.BlockSpec((q, GROUP_WIDTH), rowmap(jz)),
        small, small,
        pl.BlockSpec((1, 1, HEADS_PER_GROUP, q), lambda b, g, c: (b, g, 0, c)),
        pl.BlockSpec((None, CONV_W, GROUP_WIDTH), lmap(0)),
        pl.BlockSpec((None, CONV_W, D_STATE), lmap(wjb)),
        pl.BlockSpec((None, CONV_W, D_STATE), lmap(wjc)),
        pl.BlockSpec((None, 1, GROUP_WIDTH), lmap(0)),
        pl.BlockSpec((None, 1, D_STATE), lmap(wjb)),
        pl.BlockSpec((None, 1, D_STATE), lmap(wjc)),
        pl.BlockSpec((1, GROUP_WIDTH), wmap(0)),
        pl.BlockSpec((1, GROUP_WIDTH), wmap(0)),
    ]
    return pl.pallas_call(
        _ssd_prompt_kernel,
        grid=(bsz, SSM_GROUPS, nc),
        in_specs=in_specs,
        out_specs=[pl.BlockSpec((q, GROUP_WIDTH), lambda b, g, c: (b * nc + c, g)),
                   pl.BlockSpec((1, GROUP_WIDTH, D_STATE), lambda b, g, c: (b, g, 0))],
        out_shape=[jax.ShapeDtypeStruct((bsz * seq, d_inner), BF16),
                   jax.ShapeDtypeStruct((bsz, d_inner, D_STATE), F32)],
        scratch_shapes=[pltpu.VMEM((GROUP_WIDTH, D_STATE), F32),
                        pltpu.VMEM((SUBLANES + q, GROUP_WIDTH), F32),
                        pltpu.VMEM((SUBLANES + q, D_STATE), F32),
                        pltpu.VMEM((SUBLANES + q, D_STATE), F32)],
        compiler_params=_cparams("parallel", "parallel", "arbitrary"),
        name="ssd_prompt",
    )(zx, zx, zx, zx, dtg, acsg, acstg, conv_w, conv_w, conv_w, conv_b, conv_b, conv_b, dskx, gs)


def _ssd_sample_kernel(h_ref, xbc_ref, hist_ref, z_ref, dt_ref, acs_ref, acst_ref,
                       wc_ref, bc_ref, dsk_ref, gs_ref, *refs, n_new, aliased):
    if aliased:
        refs = refs[1:]
    o_ref, hn_ref, xp_ref, xa_ref, z16_ref = refs
    d_inner = SSM_GROUPS * GROUP_WIDTH
    xp_ref[...] = jnp.zeros_like(xp_ref)
    xp_ref[0:CONV_W - 1, :] = hist_ref[0]
    xp_ref[CONV_W - 1:CONV_W - 1 + n_new, :] = xbc_ref[0]
    xp = xp_ref[...]
    out = bc_ref[...]
    for j in range(CONV_W):
        out = out + xp[j:j + n_new] * wc_ref[j:j + 1, :]
    xa_ref[...] = jnp.zeros_like(xa_ref)
    xa_ref[0:n_new, :] = _silu(out)
    z16_ref[...] = jnp.zeros_like(z16_ref)
    z16_ref[0:n_new, :] = z_ref[0]
    dt16 = dt_ref[0]
    acs16 = acs_ref[0]
    acst = acst_ref[0]
    for g in range(SSM_GROUPS):
        cs = slice(g * GROUP_WIDTH, (g + 1) * GROUP_WIDTH)
        hs = slice(g * HEADS_PER_GROUP, (g + 1) * HEADS_PER_GROUP)
        bcol = d_inner + g * D_STATE
        ccol = d_inner + (SSM_GROUPS + g) * D_STATE
        y, hnew = _ssd_chunk(xa_ref[:, cs], xa_ref[:, bcol:bcol + D_STATE], xa_ref[:, ccol:ccol + D_STATE],
                             z16_ref[:, cs], dt16[:, hs], acs16[:, hs], acst[hs, :],
                             h_ref[0, cs, :], dsk_ref[:, cs], gs_ref[:, cs])
        for j, hj in enumerate(hnew):
            r0 = g * GROUP_WIDTH + j * SSM_HEAD_DIM
            hn_ref[0, 0, r0:r0 + SSM_HEAD_DIM, :] = hj
        o_ref[0, :, cs] = y


def ssd_sample(layer, depth, h0, xbc, hist, z, dt16, acs16, acst, conv_w, conv_b, dskx, gs, prev_states):
    _, db, hd, n = h0.shape
    n_new = xbc.shape[1]
    conv_dim = xbc.shape[2]
    d_inner = SSM_GROUPS * GROUP_WIDTH
    nh = d_inner // SSM_HEAD_DIM
    per_b = lambda *shape: pl.BlockSpec((1,) + shape, lambda b: (b,) + (0,) * len(shape))
    full = lambda *shape: pl.BlockSpec(shape, lambda b: (0,) * len(shape))
    aliased = prev_states is not None
    per_lb = lambda *shape: pl.BlockSpec((None, 1) + shape, lambda b: (layer, b) + (0,) * len(shape))
    per_l = lambda *shape: pl.BlockSpec((None,) + shape, lambda b: (layer,) + (0,) * len(shape))
    in_specs = [per_lb(hd, n), per_b(n_new, conv_dim), per_lb(CONV_W - 1, conv_dim), per_b(n_new, d_inner),
                per_b(TOK16, nh), per_b(TOK16, nh), per_b(nh, TOK16),
                per_l(CONV_W, conv_dim), per_l(1, conv_dim), full(1, d_inner), full(1, d_inner)]
    args = [h0, xbc, hist, z, dt16, acs16, acst, conv_w, conv_b, dskx, gs]
    if aliased:
        in_specs.append(pl.BlockSpec(memory_space=pl.ANY))
        args.append(prev_states)
    return pl.pallas_call(
        functools.partial(_ssd_sample_kernel, n_new=n_new, aliased=aliased),
        grid=(db,),
        in_specs=in_specs,
        out_specs=[per_b(TOK16, d_inner),
                   pl.BlockSpec((1, 1, hd, n), lambda b: (layer, b, 0, 0))],
        out_shape=[jax.ShapeDtypeStruct((db, TOK16, d_inner), F32),
                   jax.ShapeDtypeStruct((depth, db, hd, n), F32)],
        scratch_shapes=[pltpu.VMEM((SUBLANES, conv_dim), F32),
                        pltpu.VMEM((TOK16, conv_dim), F32),
                        pltpu.VMEM((TOK16, d_inner), F32)],
        input_output_aliases={len(args) - 1: 1} if aliased else {},
        compiler_params=_cparams("parallel"),
        name="ssd_sample",
    )(*args)


def _page_head(page_ref, hk):
    return page_ref[0, 0, pl.ds(hk, PAGE, stride=N_KV_HEADS), :]


def _attn_sample_kernel(pt_ref, q_ref, kn_ref, vn_ref, nb_ref, tm_ref, *refs, n_new, pps):
    del pt_ref
    k_refs, v_refs, lf_refs = refs[0:pps], refs[pps:2 * pps], refs[2 * pps:3 * pps]
    o_ref, qs_ref, knp_ref, m_ref, l_ref, acc_ref, carry_ref, r_ref = refs[3 * pps:]
    b = pl.program_id(0)
    p = pl.program_id(1)
    rows = KV_GROUP * n_new

    @pl.when((b == 0) & (p == 0))
    def _():
        knp_ref[...] = jnp.zeros_like(knp_ref)

    @pl.when(p == 0)
    def _():
        for head in range(N_HEADS):
            qs_ref[head * n_new:(head + 1) * n_new, :] = q_ref[0, :, head * HEAD_DIM:(head + 1) * HEAD_DIM]
        knp_ref[0:n_new, :] = kn_ref[0]
        m_ref[...] = jnp.full_like(m_ref, NEG_INF)
        l_ref[...] = jnp.zeros_like(l_ref)
        acc_ref[...] = jnp.zeros_like(acc_ref)
        carry_ref[...] = jnp.zeros_like(carry_ref)

    tm = tm_ref[...]
    qh = [qs_ref[hk * rows:(hk + 1) * rows, :].astype(BF16) for hk in range(N_KV_HEADS)]
    s_cols = []
    for u in range(pps):
        lft = lf_refs[u][0, 0]
        hi, mid, lo = _split3(lft)
        r16 = _dot(hi, tm) + _dot(mid, tm) + _dot(lo, tm) + carry_ref[...]
        carry_ref[...] = carry_ref[...] + jnp.sum(lft, axis=-1, keepdims=True)
        for head in range(N_HEADS):
            r_ref[head * n_new:(head + 1) * n_new, u * PAGE:(u + 1) * PAGE] = jnp.broadcast_to(
                r16[head:head + 1, :], (n_new, PAGE))
        s_cols.append(jnp.concatenate(
            [_dot_nt(qh[hk], _page_head(k_refs[u], hk).astype(BF16)) for hk in range(N_KV_HEADS)], axis=0))
    s = jnp.concatenate(s_cols, axis=1) + r_ref[...] * LOG2E
    m_old = m_ref[...]
    m_new = jnp.maximum(m_old, jnp.max(s, axis=-1, keepdims=True))
    pr = jnp.exp2(s - m_new)
    alpha = jnp.exp2(m_old - m_new)
    l_ref[...] = alpha * l_ref[...] + jnp.sum(pr, axis=-1, keepdims=True)
    m_ref[...] = m_new
    acc_old = acc_ref[...]
    for hk in range(N_KV_HEADS):
        rs = slice(hk * rows, (hk + 1) * rows)
        upd = alpha[rs, :] * acc_old[rs, :]
        for u in range(pps):
            upd = upd + _dot(pr[rs, u * PAGE:(u + 1) * PAGE].astype(BF16),
                             _page_head(v_refs[u], hk).astype(BF16))
        acc_ref[rs, :] = upd

    @pl.when(p == pl.num_programs(1) - 1)
    def _():
        sn = jnp.concatenate(
            [_dot_nt(qh[hk], knp_ref[:, hk * HEAD_DIM:(hk + 1) * HEAD_DIM].astype(BF16))
             for hk in range(N_KV_HEADS)], axis=0) + nb_ref[0] * LOG2E
        m_old = m_ref[...]
        m_new = jnp.maximum(m_old, jnp.max(sn, axis=-1, keepdims=True))
        pn = jnp.exp2(sn - m_new)
        alpha = jnp.exp2(m_old - m_new)
        l = alpha * l_ref[...] + jnp.sum(pn, axis=-1, keepdims=True)
        acc = alpha * acc_ref[...]
        vn = vn_ref[0]
        for hk in range(N_KV_HEADS):
            rs = slice(hk * rows, (hk + 1) * rows)
            upd = acc[rs, :]
            for j in range(n_new):
                upd = upd + pn[rs, j:j + 1] * vn[j:j + 1, hk * HEAD_DIM:(hk + 1) * HEAD_DIM]
            acc_ref[rs, :] = upd / l[rs, :]
        for head in range(N_HEADS):
            o_ref[0, :, head * HEAD_DIM:(head + 1) * HEAD_DIM] = acc_ref[head * n_new:(head + 1) * n_new, :]


def attn_sample(layer, page_table, q, kn, vn, nb, cache_k, cache_v, cache_lft):
    db, n_new, _ = q.shape
    n_pages = page_table.shape[1]
    pps = PAGES_PER_STEP
    kvw = N_KV_HEADS * HEAD_DIM
    n_rows = n_new * N_HEADS
    idx = np.arange(PAGE)
    tm = jnp.asarray(idx[:, None] > idx[None, :], BF16)
    pt_flat = page_table.reshape(-1)

    def page(u, ndim):
        def index_map(b, p, pt):
            return (layer, pt[b * n_pages + (n_pages - 1 - (p * pps + u))]) + (0,) * ndim
        return index_map

    per_b = lambda *shape: pl.BlockSpec((1,) + shape, lambda b, p, pt: (b,) + (0,) * len(shape))
    kv_spec = lambda u: pl.BlockSpec((1, 1, PAGE * N_KV_HEADS, HEAD_DIM), page(u, 2))
    lf_spec = lambda u: pl.BlockSpec((1, 1, N_HEADS, PAGE), page(u, 2))
    grid_spec = pltpu.PrefetchScalarGridSpec(
        num_scalar_prefetch=1,
        grid=(db, n_pages // pps),
        in_specs=[per_b(n_new, N_HEADS * HEAD_DIM), per_b(n_new, kvw), per_b(n_new, kvw),
                  per_b(n_rows, LANES),
                  pl.BlockSpec((PAGE, PAGE), lambda b, p, pt: (0, 0))]
                 + [kv_spec(u) for u in range(pps)] + [kv_spec(u) for u in range(pps)]
                 + [lf_spec(u) for u in range(pps)],
        out_specs=per_b(n_new, N_HEADS * HEAD_DIM),
        scratch_shapes=[pltpu.VMEM((n_rows, HEAD_DIM), F32),
                        pltpu.VMEM((PAGE, kvw), F32),
                        pltpu.VMEM((n_rows, 1), F32),
                        pltpu.VMEM((n_rows, 1), F32),
                        pltpu.VMEM((n_rows, HEAD_DIM), F32),
                        pltpu.VMEM((N_HEADS, 1), F32),
                        pltpu.VMEM((n_rows, pps * PAGE), F32)],
    )
    return pl.pallas_call(
        functools.partial(_attn_sample_kernel, n_new=n_new, pps=pps),
        grid_spec=grid_spec,
        out_shape=jax.ShapeDtypeStruct((db, n_new, N_HEADS * HEAD_DIM), F32),
        compiler_params=_cparams("arbitrary", "arbitrary"),
        name="attn_sample",
    )(pt_flat, q, kn, vn, nb, tm, *([cache_k] * pps), *([cache_v] * pps), *([cache_lft] * pps))


def kernel(x_prompt, x_sample, cache_k, cache_v, cache_logf, state_ssm, state_conv, page_table,
           p_prompt, p_sample, g_ffn1, w_ffn1_in, w_ffn1_out, g_mix, w_in, b_f, g_q, g_k,
           conv_w, conv_b, dt_bias, a_log, d_skip, g_ssm, w_branch_attn, w_branch_ssm, w_out,
           g_ffn2, w_ffn2_in, w_ffn2_out, g_ple, w_ple_gate, w_ple_proj):
    depth = w_in.shape[0]
    bp, seq, d_model = x_prompt.shape
    db, n_new, _ = x_sample.shape
    mp, ms = bp * seq, db * n_new
    attn_w = N_HEADS * HEAD_DIM
    kv_w = N_KV_HEADS * HEAD_DIM
    d_inner = SSM_GROUPS * GROUP_WIDTH
    conv_dim = d_inner + 2 * SSM_GROUPS * D_STATE
    ssm_heads = d_inner // SSM_HEAD_DIM
    assert KV_GROUP * n_new == SUBLANES and page_table.shape[1] % PAGES_PER_STEP == 0
    o_q, o_k, o_v = 0, attn_w, attn_w + kv_w
    o_f = o_v + kv_w
    o_z = o_f + N_HEADS
    o_x = o_z + d_inner
    o_dt = o_x + conv_dim
    o_ga = o_dt + ssm_heads
    z_col, xbc_col = 0, d_inner
    ga_col = xbc_col + conv_dim
    gb_col = ga_col + d_model
    scale = HEAD_DIM ** -0.5

    n_pool = cache_k.shape[1]
    ck = cache_k.reshape(depth, n_pool, PAGE * N_KV_HEADS, HEAD_DIM)
    cv = cache_v.reshape(depth, n_pool, PAGE * N_KV_HEADS, HEAD_DIM)
    clft = jnp.swapaxes(cache_logf, 2, 3)

    state4 = state_ssm.reshape(depth, db, d_inner, D_STATE)
    conv_b3 = conv_b.reshape(depth, 1, conv_dim)

    h = jnp.concatenate([x_prompt.reshape(mp, d_model), x_sample.reshape(ms, d_model)], axis=0)
    outs = [[] for _ in range(9)]
    states_s = None
    for i in range(depth):
        wl = w_in[i]
        w_wide = jnp.concatenate([wl[:, o_z:o_dt], wl[:, o_ga:]], axis=1)
        w_small = jnp.concatenate(
            [wl[:, o_dt:o_ga], wl[:, o_f:o_z],
             jnp.zeros((d_model, LANES - ssm_heads - N_HEADS), F32)], axis=1)
        gate_bias = jnp.concatenate(
            [dt_bias[i], b_f[i], jnp.zeros((LANES - ssm_heads - N_HEADS,), F32)]).reshape(1, LANES)
        alog_row = jnp.concatenate([a_log[i], jnp.zeros((LANES - ssm_heads,), F32)]).reshape(1, LANES)
        dskx = jnp.repeat(d_skip[i], SSM_HEAD_DIM).reshape(1, d_inner)
        gs = g_ssm[i].reshape(1, d_inner)

        act = mm_swiglu(rmsnorm(h, g_ffn1[i]), w_ffn1_in, i)
        h = mm_residual(act, w_ffn1_out, i, h, 0.5, TM)

        u = rmsnorm(h, g_mix[i])
        (q_b,) = mm_headnorm(u, w_in, i, g_q[i], [BF16], o_q, attn_w, mult=scale * LOG2E)
        k_f, k_b = mm_headnorm(u, w_in, i, g_k[i], [F32, BF16], o_k, kv_w)
        v_f, v_b = mm_plain(u, w_in, i, [F32, BF16], TN, o_v, kv_w)
        (wide,) = mm_plain(u, w_wide, i, [F32], 2 * TN)
        (small,) = mm_plain(u, w_small, i, [F32], LANES)

        dt_p, lf_p, c_p, acs_p = gates(small[:mp], gate_bias, alog_row, seq, SSM_CHUNK)
        dt_s, lf_s, c_s, acs_s = gates(small[mp:], gate_bias, alog_row, n_new, n_new)
        fcols = slice(ssm_heads, ssm_heads + N_HEADS)

        ct = c_p[:, fcols].reshape(bp, seq, N_KV_HEADS, KV_GROUP).transpose(0, 2, 3, 1)
        ct = jnp.pad(ct, ((0, 0), (0, 0), (0, SUBLANES - KV_GROUP), (0, 0)))
        o_attn_p = attn_prompt(q_b, k_b, v_b, ct, bp, seq)
        pg = lambda a: a[:, :ssm_heads].reshape(bp, seq, SSM_GROUPS, HEADS_PER_GROUP).transpose(0, 2, 1, 3)
        acsg = pg(acs_p)
        o_ssm_p, st_p = ssd_prompt(i, wide, z_col, xbc_col, pg(dt_p), acsg, acsg.transpose(0, 1, 3, 2),
                                   conv_w, conv_b3, dskx, gs, bp, seq)

        wide_s = wide[mp:].reshape(db, n_new, -1)
        xbc_s = wide_s[:, :, xbc_col:ga_col]
        tail = TOK16 - n_new
        dt16 = jnp.pad(dt_s[:, :ssm_heads].reshape(db, n_new, ssm_heads), ((0, 0), (0, tail), (0, 0)))
        acs16 = jnp.pad(acs_s[:, :ssm_heads].reshape(db, n_new, ssm_heads), ((0, 0), (0, tail), (0, 0)),
                        mode="edge")
        o_ssm_s16, states_s = ssd_sample(
            i, depth, state4, xbc_s, state_conv, wide_s[:, :, z_col:z_col + d_inner],
            dt16, acs16, acs16.transpose(0, 2, 1), conv_w, conv_b3, dskx, gs, states_s)
        o_ssm_s = o_ssm_s16[:, :n_new].reshape(ms, d_inner).astype(BF16)

        cn = c_s[:, fcols].reshape(db, n_new, N_HEADS)
        tt = np.arange(n_new)
        nbias = jnp.where((tt[None, :] <= tt[:, None])[None, None, :, :],
                          -cn.transpose(0, 2, 1)[:, :, None, :], NEG_INF)
        nbias = jnp.pad(nbias.reshape(db, N_HEADS * n_new, n_new),
                        ((0, 0), (0, 0), (0, LANES - n_new)), constant_values=NEG_INF)
        o_attn_s = attn_sample(i, page_table,
                               q_b[mp:].astype(F32).reshape(db, n_new, attn_w),
                               k_b[mp:].astype(F32).reshape(db, n_new, kv_w),
                               v_f[mp:].reshape(db, n_new, kv_w),
                               nbias, ck, cv, clft)
        o_attn_s = o_attn_s.reshape(ms, attn_w).astype(BF16)

        o_attn = jnp.concatenate([o_attn_p, o_attn_s], axis=0)
        o_ssm = jnp.concatenate([o_ssm_p, o_ssm_s], axis=0)
        merged = mm_merge(o_attn, w_branch_attn, o_ssm, w_branch_ssm, i, wide, ga_col, gb_col)
        h = mm_residual(merged, w_out, i, h, 1.0, TM_WIDE)

        act = mm_swiglu(rmsnorm(h, g_ffn2[i]), w_ffn2_in, i)
        h = mm_residual(act, w_ffn2_out, i, h, 0.5, TM)

        p_l = jnp.concatenate([p_prompt[i].reshape(mp, -1), p_sample[i].reshape(ms, -1)], axis=0).astype(BF16)
        h = mm_ple(rmsnorm(h, g_ple[i]), w_ple_gate, p_l, w_ple_proj, i, h)

        conv_p = jnp.stack([wide[(b + 1) * seq - (CONV_W - 1):(b + 1) * seq, xbc_col:ga_col] for b in range(bp)])
        layer_out = (
            k_f[:mp].reshape(bp, seq, N_KV_HEADS, HEAD_DIM), v_f[:mp].reshape(bp, seq, N_KV_HEADS, HEAD_DIM),
            lf_p[:, fcols].reshape(bp, seq, N_HEADS), conv_p,
            st_p.reshape(bp, ssm_heads, SSM_HEAD_DIM, D_STATE),
            k_f[mp:].reshape(db, n_new, N_KV_HEADS, HEAD_DIM), v_f[mp:].reshape(db, n_new, N_KV_HEADS, HEAD_DIM),
            lf_s[:, fcols].reshape(db, n_new, N_HEADS),
            jnp.concatenate([state_conv[i], xbc_s], axis=1)[:, n_new:n_new + CONV_W - 1])
        for lst, val in zip(outs, layer_out):
            lst.append(val)

    stacked = [jnp.stack(lst) for lst in outs]
    return (h[:mp].reshape(bp, seq, d_model), h[mp:].reshape(db, n_new, d_model), *stacked,
            states_s.reshape(depth, db, ssm_heads, SSM_HEAD_DIM, D_STATE))
```

```python
import functools

import numpy as np
import jax
import jax.numpy as jnp
from jax import lax
from jax.experimental import pallas as pl
from jax.experimental.pallas import tpu as pltpu

F32 = jnp.float32
BF16 = jnp.bfloat16
EPS = 1e-6
NEG_INF = float("-inf")

LANES = 128
SUBLANES = 8
VMEM_LIMIT_BYTES = 56 * 1024 * 1024

HEAD_DIM = 128
N_HEADS = 16
N_KV_HEADS = 8
KV_GROUP = N_HEADS // N_KV_HEADS
SSM_HEAD_DIM = 64
SSM_GROUPS = 8
HEADS_PER_GROUP = 8
GROUP_WIDTH = HEADS_PER_GROUP * SSM_HEAD_DIM
D_STATE = 128
CONV_W = 4
SSM_CHUNK = 128
PAGE = 128

TM = 512
TM_WIDE = 1088
TN = 512
TQ = 512
RB = 512
LOG2E = 1.4426950408889634
TOK16 = 16
PAGES_PER_STEP = 8


def _cparams(*sem):
    return pltpu.CompilerParams(dimension_semantics=sem, vmem_limit_bytes=VMEM_LIMIT_BYTES)


def _sigmoid(x):
    return 1.0 / (1.0 + jnp.exp(-x))


def _silu(x):
    return x * _sigmoid(x)


def _dot(a, b):
    return jnp.dot(a, b, preferred_element_type=F32)


def _dot_nt(a, b):
    return lax.dot_general(a, b, (((1,), (1,)), ((), ())), preferred_element_type=F32)


def _dot_tn(a, b):
    return lax.dot_general(a, b, (((0,), (0,)), ((), ())), preferred_element_type=F32)


def _split3(x):
    hi = x.astype(BF16)
    r1 = x - hi.astype(F32)
    mid = r1.astype(BF16)
    lo = (r1 - mid.astype(F32)).astype(BF16)
    return hi, mid, lo


def _rmsnorm_kernel(x_ref, g_ref, o_ref):
    x = x_ref[...]
    ms = jnp.mean(x * x, axis=-1, keepdims=True)
    o_ref[...] = (x * lax.rsqrt(ms + EPS) * g_ref[...]).astype(o_ref.dtype)


def rmsnorm(x, g):
    m, d = x.shape
    return pl.pallas_call(
        _rmsnorm_kernel,
        grid=(m // TM,),
        in_specs=[pl.BlockSpec((TM, d), lambda i: (i, 0)),
                  pl.BlockSpec((1, d), lambda i: (0, 0))],
        out_specs=pl.BlockSpec((TM, d), lambda i: (i, 0)),
        out_shape=jax.ShapeDtypeStruct((m, d), BF16),
        compiler_params=_cparams("parallel"),
        name="rmsnorm",
    )(x, g.reshape(1, d))


def _cast_weights(pairs):
    @pl.when(pl.program_id(1) == 0)
    def _():
        for w_ref, ws_ref in pairs:
            ws_ref[...] = w_ref[...].reshape(ws_ref.shape).astype(BF16)


_MM_SEM = ("parallel", "arbitrary")


def _wspec(w, layer, k, tn, j0=0):
    if w.ndim == 2:
        return pl.BlockSpec((k, tn), lambda j, i: (0, j + j0))
    return pl.BlockSpec((None, k, tn), lambda j, i: (layer, 0, j + j0))


def _swiglu_kernel(a_ref, wa_ref, wb_ref, o_ref, wsa_ref, wsb_ref):
    _cast_weights([(wa_ref, wsa_ref), (wb_ref, wsb_ref)])
    a = a_ref[...]
    o_ref[...] = (_silu(_dot(a, wsa_ref[...])) * _dot(a, wsb_ref[...])).astype(o_ref.dtype)


def mm_swiglu(a, w, layer):
    m, k = a.shape
    dff = w.shape[-1] // 2
    nj = dff // TN
    tm = TM_WIDE
    return pl.pallas_call(
        _swiglu_kernel,
        grid=(nj, m // tm),
        in_specs=[pl.BlockSpec((tm, k), lambda j, i: (i, 0)),
                  _wspec(w, layer, k, TN), _wspec(w, layer, k, TN, nj)],
        out_specs=pl.BlockSpec((tm, TN), lambda j, i: (i, j)),
        out_shape=jax.ShapeDtypeStruct((m, dff), BF16),
        scratch_shapes=[pltpu.VMEM((k, TN), BF16), pltpu.VMEM((k, TN), BF16)],
        compiler_params=_cparams(*_MM_SEM),
        name="mm_swiglu",
    )(a, w, w)


def _residual_kernel(a_ref, w_ref, r_ref, o_ref, ws_ref, *, alpha):
    _cast_weights([(w_ref, ws_ref)])
    o_ref[...] = r_ref[...] + alpha * _dot(a_ref[...], ws_ref[...])


def mm_residual(a, w, layer, res, alpha, tm):
    m, k = a.shape
    n = w.shape[-1]
    return pl.pallas_call(
        functools.partial(_residual_kernel, alpha=alpha),
        grid=(n // TN, m // tm),
        in_specs=[pl.BlockSpec((tm, k), lambda j, i: (i, 0)),
                  _wspec(w, layer, k, TN),
                  pl.BlockSpec((tm, TN), lambda j, i: (i, j))],
        out_specs=pl.BlockSpec((tm, TN), lambda j, i: (i, j)),
        out_shape=jax.ShapeDtypeStruct((m, n), F32),
        scratch_shapes=[pltpu.VMEM((k, TN), BF16)],
        compiler_params=_cparams(*_MM_SEM),
        name="mm_residual",
    )(a, w, res)


def _wtspec(layer, k, rows, row0, stride=0):
    return pl.BlockSpec((pl.Element(1), pl.Element(rows), pl.Element(k)),
                        lambda j, i: (layer, pl.multiple_of(row0 + j * stride, SUBLANES), 0))


def _plain_kernel(a_ref, w_ref, *refs):
    *o_refs, ws_ref = refs
    _cast_weights([(w_ref, ws_ref)])
    acc = _dot_nt(a_ref[...], ws_ref[...])
    for o_ref in o_refs:
        o_ref[...] = acc.astype(o_ref.dtype)


def mm_plain(a, wt, layer, out_dtypes, tn, row0, n):
    m, k = a.shape
    tm = TM_WIDE
    return pl.pallas_call(
        _plain_kernel,
        grid=(n // tn, m // tm),
        in_specs=[pl.BlockSpec((tm, k), lambda j, i: (i, 0)),
                  _wtspec(layer, k, tn, row0, tn)],
        out_specs=[pl.BlockSpec((tm, tn), lambda j, i: (i, j)) for _ in out_dtypes],
        out_shape=[jax.ShapeDtypeStruct((m, n), dt) for dt in out_dtypes],
        scratch_shapes=[pltpu.VMEM((tn, k), BF16)],
        compiler_params=_cparams(*_MM_SEM),
        name="mm_plain",
    )(a, wt)


def _gate_logits_kernel(a_ref, wdt_ref, wf_ref, o_ref, ws_ref):
    @pl.when(pl.program_id(1) == 0)
    def _():
        ndt, nf = wdt_ref.shape[1], wf_ref.shape[1]
        ws_ref[...] = jnp.zeros_like(ws_ref)
        ws_ref[0:ndt, :] = wdt_ref[0].astype(BF16)
        ws_ref[ndt:ndt + nf, :] = wf_ref[0].astype(BF16)

    o_ref[...] = _dot_nt(a_ref[...], ws_ref[...])


def mm_gate_logits(a, wt, layer, dt_row0, n_dt, f_row0, n_f):
    m, k = a.shape
    tm = TM_WIDE
    return pl.pallas_call(
        _gate_logits_kernel,
        grid=(1, m // tm),
        in_specs=[pl.BlockSpec((tm, k), lambda j, i: (i, 0)),
                  _wtspec(layer, k, n_dt, dt_row0), _wtspec(layer, k, n_f, f_row0)],
        out_specs=pl.BlockSpec((tm, LANES), lambda j, i: (i, 0)),
        out_shape=jax.ShapeDtypeStruct((m, LANES), F32),
        scratch_shapes=[pltpu.VMEM((LANES, k), BF16)],
        compiler_params=_cparams(*_MM_SEM),
        name="mm_gate_logits",
    )(a, wt, wt)


def _headnorm_kernel(a_ref, w_ref, g_ref, *refs, mult):
    *o_refs, ws_ref = refs
    _cast_weights([(w_ref, ws_ref)])
    acc = _dot_nt(a_ref[...], ws_ref[...])
    g = g_ref[...]
    for hh in range(acc.shape[1] // HEAD_DIM):
        sl = slice(hh * HEAD_DIM, (hh + 1) * HEAD_DIM)
        blk = acc[:, sl]
        ms = jnp.mean(blk * blk, axis=-1, keepdims=True)
        y = blk * lax.rsqrt(ms + EPS) * g
        for o_ref in o_refs:
            o_ref[:, sl] = (y * mult).astype(o_ref.dtype) if o_ref.dtype == BF16 else y


def mm_headnorm(a, wt, layer, g, out_dtypes, row0, n, mult=1.0):
    m, k = a.shape
    tm = TM_WIDE
    return pl.pallas_call(
        functools.partial(_headnorm_kernel, mult=mult),
        grid=(n // TN, m // tm),
        in_specs=[pl.BlockSpec((tm, k), lambda j, i: (i, 0)),
                  _wtspec(layer, k, TN, row0, TN),
                  pl.BlockSpec((1, HEAD_DIM), lambda j, i: (0, 0))],
        out_specs=[pl.BlockSpec((tm, TN), lambda j, i: (i, j)) for _ in out_dtypes],
        out_shape=[jax.ShapeDtypeStruct((m, n), dt) for dt in out_dtypes],
        scratch_shapes=[pltpu.VMEM((TN, k), BF16)],
        compiler_params=_cparams(*_MM_SEM),
        name="mm_headnorm",
    )(a, wt, g.reshape(1, HEAD_DIM))


def _merge_kernel(a1p_ref, a1s_ref, w1_ref, a2p_ref, a2s_ref, w2_ref, ga_ref, gb_ref, o_ref, ws1_ref, ws2_ref,
                  *, prompt_tiles):
    _cast_weights([(w1_ref, ws1_ref), (w2_ref, ws2_ref)])

    def emit(a1_ref, a2_ref):
        o_ref[...] = (_sigmoid(ga_ref[...]) * _dot(a1_ref[...], ws1_ref[...])
                      + _sigmoid(gb_ref[...]) * _dot(a2_ref[...], ws2_ref[...])).astype(o_ref.dtype)

    @pl.when(pl.program_id(1) < prompt_tiles)
    def _():
        emit(a1p_ref, a2p_ref)

    @pl.when(pl.program_id(1) >= prompt_tiles)
    def _():
        emit(a1s_ref, a2s_ref)


def mm_merge(a1p, a1s, w1, a2p, a2s, w2, layer, gates, ga_col, gb_col):
    mp, k1 = a1p.shape
    k2 = a2p.shape[1]
    n = w1.shape[-1]
    assert a1s.shape[0] == TM and a2s.shape[0] == TM
    npt = mp // TM
    ja, jb = ga_col // TN, gb_col // TN
    prompt = lambda k: pl.BlockSpec((TM, k), lambda j, i: (jnp.minimum(i, npt - 1), 0))
    sample = lambda k: pl.BlockSpec((TM, k), lambda j, i: (0, 0))
    return pl.pallas_call(
        functools.partial(_merge_kernel, prompt_tiles=npt),
        grid=(n // TN, npt + 1),
        in_specs=[prompt(k1), sample(k1), _wspec(w1, layer, k1, TN),
                  prompt(k2), sample(k2), _wspec(w2, layer, k2, TN),
                  pl.BlockSpec((TM, TN), lambda j, i: (i, j + ja)),
                  pl.BlockSpec((TM, TN), lambda j, i: (i, j + jb))],
        out_specs=pl.BlockSpec((TM, TN), lambda j, i: (i, j)),
        out_shape=jax.ShapeDtypeStruct((mp + TM, n), BF16),
        scratch_shapes=[pltpu.VMEM((k1, TN), BF16), pltpu.VMEM((k2, TN), BF16)],
        compiler_params=_cparams(*_MM_SEM),
        name="mm_merge",
    )(a1p, a1s, w1, a2p, a2s, w2, gates, gates)


def _ple_kernel(a_ref, wg_ref, p_ref, wp_ref, r_ref, o_ref, wsg_ref, wsp_ref):
    _cast_weights([(wg_ref, wsg_ref), (wp_ref, wsp_ref)])
    gate = _sigmoid(_dot(a_ref[...], wsg_ref[...]))
    o_ref[...] = r_ref[...] + gate * _dot(p_ref[...], wsp_ref[...])


def mm_ple(a, wg, p, wp, layer, res):
    m, k = a.shape
    kp = p.shape[1]
    n = wg.shape[-1]
    tm = TM_WIDE
    return pl.pallas_call(
        _ple_kernel,
        grid=(n // TN, m // tm),
        in_specs=[pl.BlockSpec((tm, k), lambda j, i: (i, 0)),
                  _wspec(wg, layer, k, TN),
                  pl.BlockSpec((tm, kp), lambda j, i: (i, 0)),
                  _wspec(wp, layer, kp, TN),
                  pl.BlockSpec((tm, TN), lambda j, i: (i, j))],
        out_specs=pl.BlockSpec((tm, TN), lambda j, i: (i, j)),
        out_shape=jax.ShapeDtypeStruct((m, n), F32),
        scratch_shapes=[pltpu.VMEM((k, TN), BF16), pltpu.VMEM((kp, TN), BF16)],
        compiler_params=_cparams(*_MM_SEM),
        name="mm_ple",
    )(a, wg, p, wp, res)


def _tri_cumsum(t, x):
    hi, mid, lo = _split3(x)
    return _dot(t, hi) + _dot(t, mid) + _dot(t, lo)


def _gates_kernel(x_ref, bias_ref, alog_ref, tseq_ref, tchk_ref,
                  dt_ref, lf_ref, c_ref, acs_ref, carry_ref, *, blocks_per_seq):
    i = pl.program_id(0)
    xb = x_ref[...] + bias_ref[...]
    t = jnp.log1p(jnp.exp(-jnp.abs(xb)))
    sp = jnp.maximum(xb, 0.0) + t
    ls = jnp.minimum(xb, 0.0) - t
    dt_ref[...] = sp
    lf_ref[...] = ls
    a = sp * (-jnp.exp(alog_ref[...]))
    acs_ref[...] = _tri_cumsum(tchk_ref[...], a)

    @pl.when(i % blocks_per_seq == 0)
    def _():
        carry_ref[...] = jnp.zeros_like(carry_ref)

    c = _tri_cumsum(tseq_ref[...], ls) + carry_ref[...]
    c_ref[...] = c
    carry_ref[...] = c[TM - 1:TM, :]


def gates(x, bias, alog, seq_len, chunk):
    m = x.shape[0]
    idx = np.arange(TM)
    low = idx[:, None] >= idx[None, :]
    tseq = jnp.asarray(low & (idx[:, None] // seq_len == idx[None, :] // seq_len), BF16)
    tchk = jnp.asarray(low & (idx[:, None] // chunk == idx[None, :] // chunk), BF16)
    row = pl.BlockSpec((TM, LANES), lambda i: (i, 0))
    vec = pl.BlockSpec((1, LANES), lambda i: (0, 0))
    tri = pl.BlockSpec((TM, TM), lambda i: (0, 0))
    return pl.pallas_call(
        functools.partial(_gates_kernel, blocks_per_seq=max(1, seq_len // TM)),
        grid=(m // TM,),
        in_specs=[row, vec, vec, tri, tri],
        out_specs=[row, row, row, row],
        out_shape=[jax.ShapeDtypeStruct((m, LANES), F32)] * 4,
        scratch_shapes=[pltpu.VMEM((1, LANES), F32)],
        compiler_params=_cparams("arbitrary"),
        name="gates",
    )(x, bias, alog, tseq, tchk)


def _attn_prompt_kernel(q_ref, k_ref, v_ref, ct_ref, o_ref):
    qi = pl.program_id(2)
    q0 = pl.multiple_of(qi * TQ, TQ)
    chains = [(g, rb) for g in range(KV_GROUP) for rb in range(TQ // RB)]
    crefs = [ct_ref[0, 0, g:g + 1, pl.ds(q0, LANES)][:, 0:1] for g in range(KV_GROUP)]

    def tile(g, rb, ks, nk, m, l, acc, masked):
        q = q_ref[rb * RB:(rb + 1) * RB, g * HEAD_DIM:(g + 1) * HEAD_DIM]
        kb = k_ref[pl.ds(ks, nk), :]
        vb = v_ref[pl.ds(ks, nk), :]
        s = _dot_nt(q, kb) + (crefs[g] - ct_ref[0, 0, g:g + 1, pl.ds(ks, nk)]) * LOG2E
        if masked:
            row = lax.broadcasted_iota(jnp.int32, (RB, nk), 0) + rb * RB
            col = lax.broadcasted_iota(jnp.int32, (RB, nk), 1)
            s = jnp.where(col <= row, s, NEG_INF)
        m_new = jnp.maximum(m, jnp.max(s, axis=-1, keepdims=True))
        p = jnp.exp2(s - m_new)
        alpha = jnp.exp2(m - m_new)
        l = alpha * l + jnp.sum(p, axis=-1, keepdims=True)
        acc = alpha * acc + _dot(p.astype(BF16), vb)
        return m_new, l, acc

    def body(ki, carry):
        ks = pl.multiple_of(ki * TQ, TQ)
        return tuple(tile(g, rb, ks, TQ, *carry[n], masked=False) for n, (g, rb) in enumerate(chains))

    init = tuple((jnp.full((RB, 1), NEG_INF, F32), jnp.zeros((RB, 1), F32), jnp.zeros((RB, HEAD_DIM), F32))
                 for _ in chains)
    carry = lax.fori_loop(0, qi, body, init)
    for n, (g, rb) in enumerate(chains):
        m, l, acc = tile(g, rb, q0, (rb + 1) * RB, *carry[n], masked=True)
        o_ref[rb * RB:(rb + 1) * RB, g * HEAD_DIM:(g + 1) * HEAD_DIM] = (acc / l).astype(o_ref.dtype)


def attn_prompt(q, k, v, ct, bsz, seq):
    nq = seq // TQ
    qw = KV_GROUP * HEAD_DIM
    return pl.pallas_call(
        _attn_prompt_kernel,
        grid=(bsz, N_KV_HEADS, nq),
        in_specs=[pl.BlockSpec((TQ, qw), lambda b, h, i: (b * nq + i, h)),
                  pl.BlockSpec((seq, HEAD_DIM), lambda b, h, i: (b, h)),
                  pl.BlockSpec((seq, HEAD_DIM), lambda b, h, i: (b, h)),
                  pl.BlockSpec((1, 1, SUBLANES, seq), lambda b, h, i: (b, h, 0, 0))],
        out_specs=pl.BlockSpec((TQ, qw), lambda b, h, i: (b * nq + i, h)),
        out_shape=jax.ShapeDtypeStruct((bsz * seq, N_HEADS * HEAD_DIM), BF16),
        compiler_params=_cparams("parallel", "parallel", "arbitrary"),
        name="attn_prompt",
    )(q, k, v, ct)


def _expand_heads(a):
    lane = lax.broadcasted_iota(jnp.int32, (a.shape[0], LANES), 1)
    blocks = [jnp.where(lane < SSM_HEAD_DIM, a[:, 2 * jj:2 * jj + 1], a[:, 2 * jj + 1:2 * jj + 2])
              for jj in range(HEADS_PER_GROUP // 2)]
    return jnp.concatenate(blocks, axis=1)


def _ssd_chunk(xs, bm, cm, z, dt8, acs8, acst, hprev, dsk, gs):
    q = xs.shape[0]
    dtx = _expand_heads(dt8)
    acsx = _expand_heads(acs8)
    acs_end = acsx[q - 1:q, :]
    xdt = xs * dtx
    cmb = cm.astype(BF16)
    bmb = bm.astype(BF16)
    cb = _dot_nt(cmb, bmb)
    row = lax.broadcasted_iota(jnp.int32, (q, q), 0)
    col = lax.broadcasted_iota(jnp.int32, (q, q), 1)
    causal = row >= col
    lane = lax.broadcasted_iota(jnp.int32, (q, LANES), 1)
    yblocks = []
    for jj in range(HEADS_PER_GROUP // 2):
        xblk = xdt[:, jj * LANES:(jj + 1) * LANES]
        acc = None
        for e in range(2):
            j = 2 * jj + e
            seg = acs8[:, j:j + 1] - acst[j:j + 1, :]
            lmat = jnp.exp(jnp.where(causal, seg, NEG_INF))
            mh = (cb * lmat).astype(BF16)
            keep = (lane < SSM_HEAD_DIM) if e == 0 else (lane >= SSM_HEAD_DIM)
            part = _dot(mh, jnp.where(keep, xblk, 0.0).astype(BF16))
            acc = part if acc is None else acc + part
        yblocks.append(acc)
    y_diag = jnp.concatenate(yblocks, axis=1)
    y_off = _dot_nt(cmb, hprev.astype(BF16)) * jnp.exp(acsx)
    xw = (xdt * jnp.exp(acs_end - acsx)).astype(BF16)
    states = _dot_tn(xw, bmb)
    hnew = []
    for j in range(HEADS_PER_GROUP):
        sl = slice(j * SSM_HEAD_DIM, (j + 1) * SSM_HEAD_DIM)
        hnew.append(hprev[sl, :] * jnp.exp(acst[j:j + 1, q - 1:q]) + states[sl, :])
    y = y_diag + y_off + dsk * xs
    y = y * _silu(z)
    ms = jnp.mean(y * y, axis=-1, keepdims=True)
    return y * lax.rsqrt(ms + EPS) * gs, hnew


def _conv_silu(cur_ref, buf_ref, w_ref, b_ref):
    rows = cur_ref.shape[0]
    buf_ref[SUBLANES:SUBLANES + rows, :] = cur_ref[...]
    out = b_ref[...]
    for j in range(CONV_W):
        lo = SUBLANES - (CONV_W - 1) + j
        out = out + buf_ref[lo:lo + rows, :] * w_ref[j:j + 1, :]
    buf_ref[0:SUBLANES, :] = buf_ref[rows:rows + SUBLANES, :]
    return _silu(out)


def _ssd_prompt_kernel(x_ref, b_ref, c_ref, z_ref, dt_ref, acs_ref, acst_ref,
                       wx_ref, wb_ref, wc_ref, bx_ref, bb_ref, bc_ref, dsk_ref, gs_ref,
                       o_ref, st_ref, h_ref, px_ref, pb_ref, pc_ref):
    ci = pl.program_id(2)

    @pl.when(ci == 0)
    def _():
        h_ref[...] = jnp.zeros_like(h_ref)
        for buf_ref in (px_ref, pb_ref, pc_ref):
            buf_ref[0:SUBLANES, :] = jnp.zeros((SUBLANES, buf_ref.shape[1]), F32)

    xs = _conv_silu(x_ref, px_ref, wx_ref, bx_ref)
    bm = _conv_silu(b_ref, pb_ref, wb_ref, bb_ref)
    cm = _conv_silu(c_ref, pc_ref, wc_ref, bc_ref)
    y, hnew = _ssd_chunk(xs, bm, cm, z_ref[...], dt_ref[0, 0], acs_ref[0, 0], acst_ref[0, 0],
                         h_ref[...], dsk_ref[...], gs_ref[...])
    for j, hj in enumerate(hnew):
        h_ref[j * SSM_HEAD_DIM:(j + 1) * SSM_HEAD_DIM, :] = hj
    o_ref[...] = y.astype(o_ref.dtype)

    @pl.when(ci == pl.num_programs(2) - 1)
    def _():
        st_ref[0] = h_ref[...]


def ssd_prompt(layer, zx, z_col, xbc_col, dtg, acsg, acstg, conv_w, conv_b, dskx, gs, bsz, seq):
    nc = seq // SSM_CHUNK
    d_inner = SSM_GROUPS * GROUP_WIDTH
    jz = z_col // GROUP_WIDTH
    jx = xbc_col // GROUP_WIDTH
    jb = (xbc_col + d_inner) // D_STATE
    jc = jb + SSM_GROUPS
    wjb = d_inner // D_STATE
    wjc = wjb + SSM_GROUPS
    q = SSM_CHUNK
    rowmap = lambda off: (lambda b, g, c: (b * nc + c, off + g))
    wmap = lambda off: (lambda b, g, c: (0, off + g))
    lmap = lambda off: (lambda b, g, c: (layer, 0, off + g))
    small = pl.BlockSpec((1, 1, q, HEADS_PER_GROUP), lambda b, g, c: (b, g, c, 0))
    in_specs = [
        pl.BlockSpec((q, GROUP_WIDTH), rowmap(jx)),
        pl.BlockSpec((q, D_STATE), rowmap(jb)),
        pl.BlockSpec((q, D_STATE), rowmap(jc)),
        pl.BlockSpec((q, GROUP_WIDTH), rowmap(jz)),
        small, small,
        pl.BlockSpec((1, 1, HEADS_PER_GROUP, q), lambda b, g, c: (b, g, 0, c)),
        pl.BlockSpec((None, CONV_W, GROUP_WIDTH), lmap(0)),
        pl.BlockSpec((None, CONV_W, D_STATE), lmap(wjb)),
        pl.BlockSpec((None, CONV_W, D_STATE), lmap(wjc)),
        pl.BlockSpec((None, 1, GROUP_WIDTH), lmap(0)),
        pl.BlockSpec((None, 1, D_STATE), lmap(wjb)),
        pl.BlockSpec((None, 1, D_STATE), lmap(wjc)),
        pl.BlockSpec((1, GROUP_WIDTH), wmap(0)),
        pl.BlockSpec((1, GROUP_WIDTH), wmap(0)),
    ]
    return pl.pallas_call(
        _ssd_prompt_kernel,
        grid=(bsz, SSM_GROUPS, nc),
        in_specs=in_specs,
        out_specs=[pl.BlockSpec((q, GROUP_WIDTH), lambda b, g, c: (b * nc + c, g)),
                   pl.BlockSpec((1, GROUP_WIDTH, D_STATE), lambda b, g, c: (b, g, 0))],
        out_shape=[jax.ShapeDtypeStruct((bsz * seq, d_inner), BF16),
                   jax.ShapeDtypeStruct((bsz, d_inner, D_STATE), F32)],
        scratch_shapes=[pltpu.VMEM((GROUP_WIDTH, D_STATE), F32),
                        pltpu.VMEM((SUBLANES + q, GROUP_WIDTH), F32),
                        pltpu.VMEM((SUBLANES + q, D_STATE), F32),
                        pltpu.VMEM((SUBLANES + q, D_STATE), F32)],
        compiler_params=_cparams("parallel", "parallel", "arbitrary"),
        name="ssd_prompt",
    )(zx, zx, zx, zx, dtg, acsg, acstg, conv_w, conv_w, conv_w, conv_b, conv_b, conv_b, dskx, gs)


def _ssd_sample_kernel(h_ref, xbc_ref, hist_ref, z_ref, dt_ref, acs_ref, acst_ref,
                       wc_ref, bc_ref, dsk_ref, gs_ref, *refs, n_new, aliased):
    if aliased:
        refs = refs[1:]
    o_ref, hn_ref, xp_ref, xa_ref, z16_ref = refs
    d_inner = SSM_GROUPS * GROUP_WIDTH
    xp_ref[...] = jnp.zeros_like(xp_ref)
    xp_ref[0:CONV_W - 1, :] = hist_ref[0]
    xp_ref[CONV_W - 1:CONV_W - 1 + n_new, :] = xbc_ref[0]
    xp = xp_ref[...]
    out = bc_ref[...]
    for j in range(CONV_W):
        out = out + xp[j:j + n_new] * wc_ref[j:j + 1, :]
    xa_ref[...] = jnp.zeros_like(xa_ref)
    xa_ref[0:n_new, :] = _silu(out)
    z16_ref[...] = jnp.zeros_like(z16_ref)
    z16_ref[0:n_new, :] = z_ref[0]
    dt16 = dt_ref[0]
    acs16 = acs_ref[0]
    acst = acst_ref[0]
    for g in range(SSM_GROUPS):
        cs = slice(g * GROUP_WIDTH, (g + 1) * GROUP_WIDTH)
        hs = slice(g * HEADS_PER_GROUP, (g + 1) * HEADS_PER_GROUP)
        bcol = d_inner + g * D_STATE
        ccol = d_inner + (SSM_GROUPS + g) * D_STATE
        y, hnew = _ssd_chunk(xa_ref[:, cs], xa_ref[:, bcol:bcol + D_STATE], xa_ref[:, ccol:ccol + D_STATE],
                             z16_ref[:, cs], dt16[:, hs], acs16[:, hs], acst[hs, :],
                             h_ref[0, cs, :], dsk_ref[:, cs], gs_ref[:, cs])
        for j, hj in enumerate(hnew):
            r0 = g * GROUP_WIDTH + j * SSM_HEAD_DIM
            hn_ref[0, 0, r0:r0 + SSM_HEAD_DIM, :] = hj
        o_ref[0, :, cs] = y


def ssd_sample(layer, depth, h0, xbc, hist, z, dt16, acs16, acst, conv_w, conv_b, dskx, gs, prev_states):
    _, db, hd, n = h0.shape
    n_new = xbc.shape[1]
    conv_dim = xbc.shape[2]
    d_inner = SSM_GROUPS * GROUP_WIDTH
    nh = d_inner // SSM_HEAD_DIM
    per_b = lambda *shape: pl.BlockSpec((1,) + shape, lambda b: (b,) + (0,) * len(shape))
    full = lambda *shape: pl.BlockSpec(shape, lambda b: (0,) * len(shape))
    aliased = prev_states is not None
    per_lb = lambda *shape: pl.BlockSpec((None, 1) + shape, lambda b: (layer, b) + (0,) * len(shape))
    per_l = lambda *shape: pl.BlockSpec((None,) + shape, lambda b: (layer,) + (0,) * len(shape))
    in_specs = [per_lb(hd, n), per_b(n_new, conv_dim), per_lb(CONV_W - 1, conv_dim), per_b(n_new, d_inner),
                per_b(TOK16, nh), per_b(TOK16, nh), per_b(nh, TOK16),
                per_l(CONV_W, conv_dim), per_l(1, conv_dim), full(1, d_inner), full(1, d_inner)]
    args = [h0, xbc, hist, z, dt16, acs16, acst, conv_w, conv_b, dskx, gs]
    if aliased:
        in_specs.append(pl.BlockSpec(memory_space=pl.ANY))
        args.append(prev_states)
    return pl.pallas_call(
        functools.partial(_ssd_sample_kernel, n_new=n_new, aliased=aliased),
        grid=(db,),
        in_specs=in_specs,
        out_specs=[per_b(TOK16, d_inner),
                   pl.BlockSpec((1, 1, hd, n), lambda b: (layer, b, 0, 0))],
        out_shape=[jax.ShapeDtypeStruct((db, TOK16, d_inner), F32),
                   jax.ShapeDtypeStruct((depth, db, hd, n), F32)],
        scratch_shapes=[pltpu.VMEM((SUBLANES, conv_dim), F32),
                        pltpu.VMEM((TOK16, conv_dim), F32),
                        pltpu.VMEM((TOK16, d_inner), F32)],
        input_output_aliases={len(args) - 1: 1} if aliased else {},
        compiler_params=_cparams("parallel"),
        name="ssd_sample",
    )(*args)


def _page_head(page_ref, hk):
    return page_ref[0, 0, pl.ds(hk, PAGE, stride=N_KV_HEADS), :]


def _attn_sample_kernel(pt_ref, q_ref, kn_ref, vn_ref, nb_ref, tm_ref, *refs, n_new, pps):
    del pt_ref
    k_refs, v_refs, lf_refs = refs[0:pps], refs[pps:2 * pps], refs[2 * pps:3 * pps]
    o_ref, qs_ref, knp_ref, m_ref, l_ref, acc_ref, carry_ref, r_ref = refs[3 * pps:]
    b = pl.program_id(0)
    p = pl.program_id(1)
    rows = KV_GROUP * n_new

    @pl.when((b == 0) & (p == 0))
    def _():
        knp_ref[...] = jnp.zeros_like(knp_ref)

    @pl.when(p == 0)
    def _():
        for head in range(N_HEADS):
            qs_ref[head * n_new:(head + 1) * n_new, :] = q_ref[0, :, head * HEAD_DIM:(head + 1) * HEAD_DIM]
        knp_ref[0:n_new, :] = kn_ref[0]
        m_ref[...] = jnp.full_like(m_ref, NEG_INF)
        l_ref[...] = jnp.zeros_like(l_ref)
        acc_ref[...] = jnp.zeros_like(acc_ref)
        carry_ref[...] = jnp.zeros_like(carry_ref)

    tm = tm_ref[...]
    qh = [qs_ref[hk * rows:(hk + 1) * rows, :].astype(BF16) for hk in range(N_KV_HEADS)]
    s_cols = []
    for u in range(pps):
        lft = lf_refs[u][0, 0]
        hi, mid, lo = _split3(lft)
        r16 = _dot(hi, tm) + _dot(mid, tm) + _dot(lo, tm) + carry_ref[...]
        carry_ref[...] = carry_ref[...] + jnp.sum(lft, axis=-1, keepdims=True)
        for head in range(N_HEADS):
            r_ref[head * n_new:(head + 1) * n_new, u * PAGE:(u + 1) * PAGE] = jnp.broadcast_to(
                r16[head:head + 1, :], (n_new, PAGE))
        s_cols.append(jnp.concatenate(
            [_dot_nt(qh[hk], _page_head(k_refs[u], hk).astype(BF16)) for hk in range(N_KV_HEADS)], axis=0))
    s = jnp.concatenate(s_cols, axis=1) + r_ref[...] * LOG2E
    m_old = m_ref[...]
    m_new = jnp.maximum(m_old, jnp.max(s, axis=-1, keepdims=True))
    pr = jnp.exp2(s - m_new)
    alpha = jnp.exp2(m_old - m_new)
    l_ref[...] = alpha * l_ref[...] + jnp.sum(pr, axis=-1, keepdims=True)
    m_ref[...] = m_new
    acc_old = acc_ref[...]
    for hk in range(N_KV_HEADS):
        rs = slice(hk * rows, (hk + 1) * rows)
        upd = alpha[rs, :] * acc_old[rs, :]
        for u in range(pps):
            upd = upd + _dot(pr[rs, u * PAGE:(u + 1) * PAGE].astype(BF16),
                             _page_head(v_refs[u], hk).astype(BF16))
        acc_ref[rs, :] = upd

    @pl.when(p == pl.num_programs(1) - 1)
    def _():
        sn = jnp.concatenate(
            [_dot_nt(qh[hk], knp_ref[:, hk * HEAD_DIM:(hk + 1) * HEAD_DIM].astype(BF16))
             for hk in range(N_KV_HEADS)], axis=0) + nb_ref[0] * LOG2E
        m_old = m_ref[...]
        m_new = jnp.maximum(m_old, jnp.max(sn, axis=-1, keepdims=True))
        pn = jnp.exp2(sn - m_new)
        alpha = jnp.exp2(m_old - m_new)
        l = alpha * l_ref[...] + jnp.sum(pn, axis=-1, keepdims=True)
        acc = alpha * acc_ref[...]
        vn = vn_ref[0]
        for hk in range(N_KV_HEADS):
            rs = slice(hk * rows, (hk + 1) * rows)
            upd = acc[rs, :]
            for j in range(n_new):
                upd = upd + pn[rs, j:j + 1] * vn[j:j + 1, hk * HEAD_DIM:(hk + 1) * HEAD_DIM]
            acc_ref[rs, :] = upd / l[rs, :]
        for head in range(N_HEADS):
            o_ref[0, :, head * HEAD_DIM:(head + 1) * HEAD_DIM] = acc_ref[head * n_new:(head + 1) * n_new, :]


def attn_sample(layer, page_table, q, kn, vn, nb, cache_k, cache_v, cache_lft):
    db, n_new, _ = q.shape
    n_pages = page_table.shape[1]
    pps = PAGES_PER_STEP
    kvw = N_KV_HEADS * HEAD_DIM
    n_rows = n_new * N_HEADS
    idx = np.arange(PAGE)
    tm = jnp.asarray(idx[:, None] > idx[None, :], BF16)
    pt_flat = page_table.reshape(-1)

    def page(u, ndim):
        def index_map(b, p, pt):
            return (layer, pt[b * n_pages + (n_pages - 1 - (p * pps + u))]) + (0,) * ndim
        return index_map

    per_b = lambda *shape: pl.BlockSpec((1,) + shape, lambda b, p, pt: (b,) + (0,) * len(shape))
    kv_spec = lambda u: pl.BlockSpec((1, 1, PAGE * N_KV_HEADS, HEAD_DIM), page(u, 2))
    lf_spec = lambda u: pl.BlockSpec((1, 1, N_HEADS, PAGE), page(u, 2))
    grid_spec = pltpu.PrefetchScalarGridSpec(
        num_scalar_prefetch=1,
        grid=(db, n_pages // pps),
        in_specs=[per_b(n_new, N_HEADS * HEAD_DIM), per_b(n_new, kvw), per_b(n_new, kvw),
                  per_b(n_rows, LANES),
                  pl.BlockSpec((PAGE, PAGE), lambda b, p, pt: (0, 0))]
                 + [kv_spec(u) for u in range(pps)] + [kv_spec(u) for u in range(pps)]
                 + [lf_spec(u) for u in range(pps)],
        out_specs=per_b(n_new, N_HEADS * HEAD_DIM),
        scratch_shapes=[pltpu.VMEM((n_rows, HEAD_DIM), F32),
                        pltpu.VMEM((PAGE, kvw), F32),
                        pltpu.VMEM((n_rows, 1), F32),
                        pltpu.VMEM((n_rows, 1), F32),
                        pltpu.VMEM((n_rows, HEAD_DIM), F32),
                        pltpu.VMEM((N_HEADS, 1), F32),
                        pltpu.VMEM((n_rows, pps * PAGE), F32)],
    )
    return pl.pallas_call(
        functools.partial(_attn_sample_kernel, n_new=n_new, pps=pps),
        grid_spec=grid_spec,
        out_shape=jax.ShapeDtypeStruct((db, n_new, N_HEADS * HEAD_DIM), F32),
        compiler_params=_cparams("arbitrary", "arbitrary"),
        name="attn_sample",
    )(pt_flat, q, kn, vn, nb, tm, *([cache_k] * pps), *([cache_v] * pps), *([cache_lft] * pps))


def kernel(x_prompt, x_sample, cache_k, cache_v, cache_logf, state_ssm, state_conv, page_table,
           p_prompt, p_sample, g_ffn1, w_ffn1_in, w_ffn1_out, g_mix, w_in, b_f, g_q, g_k,
           conv_w, conv_b, dt_bias, a_log, d_skip, g_ssm, w_branch_attn, w_branch_ssm, w_out,
           g_ffn2, w_ffn2_in, w_ffn2_out, g_ple, w_ple_gate, w_ple_proj):
    depth = w_in.shape[0]
    bp, seq, d_model = x_prompt.shape
    db, n_new, _ = x_sample.shape
    mp, ms = bp * seq, db * n_new
    attn_w = N_HEADS * HEAD_DIM
    kv_w = N_KV_HEADS * HEAD_DIM
    d_inner = SSM_GROUPS * GROUP_WIDTH
    conv_dim = d_inner + 2 * SSM_GROUPS * D_STATE
    ssm_heads = d_inner // SSM_HEAD_DIM
    assert KV_GROUP * n_new == SUBLANES and page_table.shape[1] % PAGES_PER_STEP == 0
    o_q, o_k, o_v = 0, attn_w, attn_w + kv_w
    o_f = o_v + kv_w
    o_z = o_f + N_HEADS
    o_x = o_z + d_inner
    o_dt = o_x + conv_dim
    o_ga = o_dt + ssm_heads
    z_col, xbc_col = 0, d_inner
    ga_col, gb_col = 0, d_model
    scale = HEAD_DIM ** -0.5

    n_pool = cache_k.shape[1]
    ck = cache_k.reshape(depth, n_pool, PAGE * N_KV_HEADS, HEAD_DIM)
    cv = cache_v.reshape(depth, n_pool, PAGE * N_KV_HEADS, HEAD_DIM)
    clft = jnp.swapaxes(cache_logf, 2, 3)

    state4 = state_ssm.reshape(depth, db, d_inner, D_STATE)
    conv_b3 = conv_b.reshape(depth, 1, conv_dim)

    h = jnp.concatenate([x_prompt.reshape(mp, d_model), x_sample.reshape(ms, d_model)], axis=0)
    outs = [[] for _ in range(9)]
    states_s = None
    wt = jnp.swapaxes(w_in, 1, 2)
    for i in range(depth):
        gate_bias = jnp.concatenate(
            [dt_bias[i], b_f[i], jnp.zeros((LANES - ssm_heads - N_HEADS,), F32)]).reshape(1, LANES)
        alog_row = jnp.concatenate([a_log[i], jnp.zeros((LANES - ssm_heads,), F32)]).reshape(1, LANES)
        dskx = jnp.repeat(d_skip[i], SSM_HEAD_DIM).reshape(1, d_inner)
        gs = g_ssm[i].reshape(1, d_inner)

        act = mm_swiglu(rmsnorm(h, g_ffn1[i]), w_ffn1_in, i)
        h = mm_residual(act, w_ffn1_out, i, h, 0.5, TM)

        u = rmsnorm(h, g_mix[i])
        (q_b,) = mm_headnorm(u, wt, i, g_q[i], [BF16], o_q, attn_w, mult=scale * LOG2E)
        k_f, k_b = mm_headnorm(u, wt, i, g_k[i], [F32, BF16], o_k, kv_w)
        v_f, v_b = mm_plain(u, wt, i, [F32, BF16], TN, o_v, kv_w)
        (wide,) = mm_plain(u, wt, i, [F32], 2 * TN, o_z, d_inner + conv_dim)
        (gate_pre,) = mm_plain(u, wt, i, [F32], 2 * TN, o_ga, 2 * d_model)
        small = mm_gate_logits(u, wt, i, o_dt, ssm_heads, o_f, N_HEADS)

        dt_p, lf_p, c_p, acs_p = gates(small[:mp], gate_bias, alog_row, seq, SSM_CHUNK)
        dt_s, lf_s, c_s, acs_s = gates(small[mp:], gate_bias, alog_row, n_new, n_new)
        fcols = slice(ssm_heads, ssm_heads + N_HEADS)

        ct = c_p[:, fcols].reshape(bp, seq, N_KV_HEADS, KV_GROUP).transpose(0, 2, 3, 1)
        ct = jnp.pad(ct, ((0, 0), (0, 0), (0, SUBLANES - KV_GROUP), (0, 0)))
        o_attn_p = attn_prompt(q_b, k_b, v_b, ct, bp, seq)
        pg = lambda a: a[:, :ssm_heads].reshape(bp, seq, SSM_GROUPS, HEADS_PER_GROUP).transpose(0, 2, 1, 3)
        acsg = pg(acs_p)
        o_ssm_p, st_p = ssd_prompt(i, wide, z_col, xbc_col, pg(dt_p), acsg, acsg.transpose(0, 1, 3, 2),
                                   conv_w, conv_b3, dskx, gs, bp, seq)

        wide_s = wide[mp:].reshape(db, n_new, -1)
        xbc_s = wide_s[:, :, xbc_col:xbc_col + conv_dim]
        tail = TOK16 - n_new
        dt16 = jnp.pad(dt_s[:, :ssm_heads].reshape(db, n_new, ssm_heads), ((0, 0), (0, tail), (0, 0)))
        acs16 = jnp.pad(acs_s[:, :ssm_heads].reshape(db, n_new, ssm_heads), ((0, 0), (0, tail), (0, 0)),
                        mode="edge")
        o_ssm_s16, states_s = ssd_sample(
            i, depth, state4, xbc_s, state_conv, wide_s[:, :, z_col:z_col + d_inner],
            dt16, acs16, acs16.transpose(0, 2, 1), conv_w, conv_b3, dskx, gs, states_s)
        o_ssm_s = o_ssm_s16[:, :n_new].reshape(ms, d_inner).astype(BF16)

        cn = c_s[:, fcols].reshape(db, n_new, N_HEADS)
        tt = np.arange(n_new)
        nbias = jnp.where((tt[None, :] <= tt[:, None])[None, None, :, :],
                          -cn.transpose(0, 2, 1)[:, :, None, :], NEG_INF)
        nbias = jnp.pad(nbias.reshape(db, N_HEADS * n_new, n_new),
                        ((0, 0), (0, 0), (0, LANES - n_new)), constant_values=NEG_INF)
        o_attn_s = attn_sample(i, page_table,
                               q_b[mp:].astype(F32).reshape(db, n_new, attn_w),
                               k_b[mp:].astype(F32).reshape(db, n_new, kv_w),
                               v_f[mp:].reshape(db, n_new, kv_w),
                               nbias, ck, cv, clft)
        o_attn_s = o_attn_s.reshape(ms, attn_w).astype(BF16)

        merged = mm_merge(o_attn_p, o_attn_s, w_branch_attn, o_ssm_p, o_ssm_s, w_branch_ssm, i,
                          gate_pre, ga_col, gb_col)
        h = mm_residual(merged, w_out, i, h, 1.0, TM_WIDE)

        act = mm_swiglu(rmsnorm(h, g_ffn2[i]), w_ffn2_in, i)
        h = mm_residual(act, w_ffn2_out, i, h, 0.5, TM)

        p_l = jnp.concatenate([p_prompt[i].reshape(mp, -1), p_sample[i].reshape(ms, -1)], axis=0).astype(BF16)
        h = mm_ple(rmsnorm(h, g_ple[i]), w_ple_gate, p_l, w_ple_proj, i, h)

        conv_p = jnp.stack([wide[(b + 1) * seq - (CONV_W - 1):(b + 1) * seq, xbc_col:xbc_col + conv_dim]
                            for b in range(bp)])
        layer_out = (
            k_f[:mp].reshape(bp, seq, N_KV_HEADS, HEAD_DIM), v_f[:mp].reshape(bp, seq, N_KV_HEADS, HEAD_DIM),
            lf_p[:, fcols].reshape(bp, seq, N_HEADS), conv_p,
            st_p.reshape(bp, ssm_heads, SSM_HEAD_DIM, D_STATE),
            k_f[mp:].reshape(db, n_new, N_KV_HEADS, HEAD_DIM), v_f[mp:].reshape(db, n_new, N_KV_HEADS, HEAD_DIM),
            lf_s[:, fcols].reshape(db, n_new, N_HEADS),
            jnp.concatenate([state_conv[i], xbc_s], axis=1)[:, n_new:n_new + CONV_W - 1])
        for lst, val in zip(outs, layer_out):
            lst.append(val)

    stacked = [jnp.stack(lst) for lst in outs]
    return (h[:mp].reshape(bp, seq, d_model), h[mp:].reshape(db, n_new, d_model), *stacked,
            states_s.reshape(depth, db, ssm_heads, SSM_HEAD_DIM, D_STATE))
```

```python
import functools

import numpy as np
import jax
import jax.numpy as jnp
from jax import lax
from jax.experimental import pallas as pl
from jax.experimental.pallas import tpu as pltpu

F32 = jnp.float32
BF16 = jnp.bfloat16
EPS = 1e-6
NEG_INF = float("-inf")

LANES = 128
SUBLANES = 8
VMEM_LIMIT_BYTES = 56 * 1024 * 1024

HEAD_DIM = 128
N_HEADS = 16
N_KV_HEADS = 8
KV_GROUP = N_HEADS // N_KV_HEADS
SSM_HEAD_DIM = 64
SSM_GROUPS = 8
HEADS_PER_GROUP = 8
GROUP_WIDTH = HEADS_PER_GROUP * SSM_HEAD_DIM
D_STATE = 128
CONV_W = 4
SSM_CHUNK = 128
PAGE = 128

TM = 512
TM_WIDE = 1088
TN = 512
TQ = 1024
RB = 1024
LOG2E = 1.4426950408889634
TOK16 = 16
PAGES_PER_STEP = 8


def _cparams(*sem):
    return pltpu.CompilerParams(dimension_semantics=sem, vmem_limit_bytes=VMEM_LIMIT_BYTES)


def _sigmoid(x):
    return 0.5 + 0.5 * jnp.tanh(0.5 * x)


def _silu(x):
    return x * _sigmoid(x)


def _dot(a, b):
    return jnp.dot(a, b, preferred_element_type=F32)


def _dot_nt(a, b):
    return lax.dot_general(a, b, (((1,), (1,)), ((), ())), preferred_element_type=F32)


def _dot_tn(a, b):
    return lax.dot_general(a, b, (((0,), (0,)), ((), ())), preferred_element_type=F32)


def _split3(x):
    hi = x.astype(BF16)
    r1 = x - hi.astype(F32)
    mid = r1.astype(BF16)
    lo = (r1 - mid.astype(F32)).astype(BF16)
    return hi, mid, lo


def _rmsnorm_kernel(x_ref, g_ref, o_ref):
    x = x_ref[...]
    ms = jnp.mean(x * x, axis=-1, keepdims=True)
    o_ref[...] = (x * lax.rsqrt(ms + EPS) * g_ref[...]).astype(o_ref.dtype)


def rmsnorm(x, g):
    m, d = x.shape
    return pl.pallas_call(
        _rmsnorm_kernel,
        grid=(m // TM,),
        in_specs=[pl.BlockSpec((TM, d), lambda i: (i, 0)),
                  pl.BlockSpec((1, d), lambda i: (0, 0))],
        out_specs=pl.BlockSpec((TM, d), lambda i: (i, 0)),
        out_shape=jax.ShapeDtypeStruct((m, d), BF16),
        compiler_params=_cparams("parallel"),
        name="rmsnorm",
    )(x, g.reshape(1, d))


def _cast_weights(pairs):
    @pl.when(pl.program_id(1) == 0)
    def _():
        for w_ref, ws_ref in pairs:
            ws_ref[...] = w_ref[...].reshape(ws_ref.shape).astype(BF16)


_MM_SEM = ("parallel", "arbitrary")


def _wspec(w, layer, k, tn, j0=0):
    if w.ndim == 2:
        return pl.BlockSpec((k, tn), lambda j, i: (0, j + j0))
    return pl.BlockSpec((None, k, tn), lambda j, i: (layer, 0, j + j0))


def _swiglu_kernel(a_ref, wa_ref, wb_ref, o_ref, wsa_ref, wsb_ref):
    _cast_weights([(wa_ref, wsa_ref), (wb_ref, wsb_ref)])
    a = a_ref[...]
    o_ref[...] = (_silu(_dot(a, wsa_ref[...])) * _dot(a, wsb_ref[...])).astype(o_ref.dtype)


def mm_swiglu(a, w, layer):
    m, k = a.shape
    dff = w.shape[-1] // 2
    nj = dff // TN
    tm = TM_WIDE
    return pl.pallas_call(
        _swiglu_kernel,
        grid=(nj, m // tm),
        in_specs=[pl.BlockSpec((tm, k), lambda j, i: (i, 0)),
                  _wspec(w, layer, k, TN), _wspec(w, layer, k, TN, nj)],
        out_specs=pl.BlockSpec((tm, TN), lambda j, i: (i, j)),
        out_shape=jax.ShapeDtypeStruct((m, dff), BF16),
        scratch_shapes=[pltpu.VMEM((k, TN), BF16), pltpu.VMEM((k, TN), BF16)],
        compiler_params=_cparams(*_MM_SEM),
        name="mm_swiglu",
    )(a, w, w)


def _residual_kernel(a_ref, w_ref, r_ref, o_ref, ws_ref, *, alpha):
    _cast_weights([(w_ref, ws_ref)])
    o_ref[...] = r_ref[...] + alpha * _dot(a_ref[...], ws_ref[...])


def mm_residual(a, w, layer, res, alpha, tm):
    m, k = a.shape
    n = w.shape[-1]
    return pl.pallas_call(
        functools.partial(_residual_kernel, alpha=alpha),
        grid=(n // TN, m // tm),
        in_specs=[pl.BlockSpec((tm, k), lambda j, i: (i, 0)),
                  _wspec(w, layer, k, TN),
                  pl.BlockSpec((tm, TN), lambda j, i: (i, j))],
        out_specs=pl.BlockSpec((tm, TN), lambda j, i: (i, j)),
        out_shape=jax.ShapeDtypeStruct((m, n), F32),
        scratch_shapes=[pltpu.VMEM((k, TN), BF16)],
        compiler_params=_cparams(*_MM_SEM),
        name="mm_residual",
    )(a, w, res)


def _wtspec(layer, k, rows, row0, stride=0):
    return pl.BlockSpec((pl.Element(1), pl.Element(rows), pl.Element(k)),
                        lambda j, i: (layer, pl.multiple_of(row0 + j * stride, SUBLANES), 0))


def _plain_kernel(a_ref, w_ref, *refs):
    *o_refs, ws_ref = refs
    _cast_weights([(w_ref, ws_ref)])
    acc = _dot_nt(a_ref[...], ws_ref[...])
    for o_ref in o_refs:
        o_ref[...] = acc.astype(o_ref.dtype)


def mm_plain(a, wt, layer, out_dtypes, tn, row0, n):
    m, k = a.shape
    tm = TM_WIDE
    return pl.pallas_call(
        _plain_kernel,
        grid=(n // tn, m // tm),
        in_specs=[pl.BlockSpec((tm, k), lambda j, i: (i, 0)),
                  _wtspec(layer, k, tn, row0, tn)],
        out_specs=[pl.BlockSpec((tm, tn), lambda j, i: (i, j)) for _ in out_dtypes],
        out_shape=[jax.ShapeDtypeStruct((m, n), dt) for dt in out_dtypes],
        scratch_shapes=[pltpu.VMEM((tn, k), BF16)],
        compiler_params=_cparams(*_MM_SEM),
        name="mm_plain",
    )(a, wt)


def _gate_logits_kernel(a_ref, wdt_ref, wf_ref, o_ref, ws_ref):
    @pl.when(pl.program_id(1) == 0)
    def _():
        ndt, nf = wdt_ref.shape[1], wf_ref.shape[1]
        ws_ref[...] = jnp.zeros_like(ws_ref)
        ws_ref[0:ndt, :] = wdt_ref[0].astype(BF16)
        ws_ref[ndt:ndt + nf, :] = wf_ref[0].astype(BF16)

    o_ref[...] = _dot_nt(a_ref[...], ws_ref[...])


def mm_gate_logits(a, wt, layer, dt_row0, n_dt, f_row0, n_f):
    m, k = a.shape
    tm = TM_WIDE
    return pl.pallas_call(
        _gate_logits_kernel,
        grid=(1, m // tm),
        in_specs=[pl.BlockSpec((tm, k), lambda j, i: (i, 0)),
                  _wtspec(layer, k, n_dt, dt_row0), _wtspec(layer, k, n_f, f_row0)],
        out_specs=pl.BlockSpec((tm, LANES), lambda j, i: (i, 0)),
        out_shape=jax.ShapeDtypeStruct((m, LANES), F32),
        scratch_shapes=[pltpu.VMEM((LANES, k), BF16)],
        compiler_params=_cparams(*_MM_SEM),
        name="mm_gate_logits",
    )(a, wt, wt)


def _headnorm_kernel(a_ref, w_ref, g_ref, *refs, mult):
    *o_refs, ws_ref = refs
    _cast_weights([(w_ref, ws_ref)])
    acc = _dot_nt(a_ref[...], ws_ref[...])
    g = g_ref[...]
    for hh in range(acc.shape[1] // HEAD_DIM):
        sl = slice(hh * HEAD_DIM, (hh + 1) * HEAD_DIM)
        blk = acc[:, sl]
        ms = jnp.mean(blk * blk, axis=-1, keepdims=True)
        y = blk * lax.rsqrt(ms + EPS) * g
        for o_ref in o_refs:
            o_ref[:, sl] = (y * mult).astype(o_ref.dtype) if o_ref.dtype == BF16 else y


def mm_headnorm(a, wt, layer, g, out_dtypes, row0, n, mult=1.0):
    m, k = a.shape
    tm = TM_WIDE
    return pl.pallas_call(
        functools.partial(_headnorm_kernel, mult=mult),
        grid=(n // TN, m // tm),
        in_specs=[pl.BlockSpec((tm, k), lambda j, i: (i, 0)),
                  _wtspec(layer, k, TN, row0, TN),
                  pl.BlockSpec((1, HEAD_DIM), lambda j, i: (0, 0))],
        out_specs=[pl.BlockSpec((tm, TN), lambda j, i: (i, j)) for _ in out_dtypes],
        out_shape=[jax.ShapeDtypeStruct((m, n), dt) for dt in out_dtypes],
        scratch_shapes=[pltpu.VMEM((TN, k), BF16)],
        compiler_params=_cparams(*_MM_SEM),
        name="mm_headnorm",
    )(a, wt, g.reshape(1, HEAD_DIM))


def _merge_kernel(a1p_ref, a1s_ref, w1_ref, a2p_ref, a2s_ref, w2_ref, ga_ref, gb_ref, o_ref, ws1_ref, ws2_ref,
                  *, prompt_tiles):
    _cast_weights([(w1_ref, ws1_ref), (w2_ref, ws2_ref)])

    def emit(a1_ref, a2_ref):
        o_ref[...] = (_sigmoid(ga_ref[...]) * _dot(a1_ref[...], ws1_ref[...])
                      + _sigmoid(gb_ref[...]) * _dot(a2_ref[...], ws2_ref[...])).astype(o_ref.dtype)

    @pl.when(pl.program_id(1) < prompt_tiles)
    def _():
        emit(a1p_ref, a2p_ref)

    @pl.when(pl.program_id(1) >= prompt_tiles)
    def _():
        emit(a1s_ref, a2s_ref)


def mm_merge(a1p, a1s, w1, a2p, a2s, w2, layer, gates, ga_col, gb_col):
    mp, k1 = a1p.shape
    k2 = a2p.shape[1]
    n = w1.shape[-1]
    assert a1s.shape[0] == TM and a2s.shape[0] == TM
    npt = mp // TM
    ja, jb = ga_col // TN, gb_col // TN
    prompt = lambda k: pl.BlockSpec((TM, k), lambda j, i: (jnp.minimum(i, npt - 1), 0))
    sample = lambda k: pl.BlockSpec((TM, k), lambda j, i: (0, 0))
    return pl.pallas_call(
        functools.partial(_merge_kernel, prompt_tiles=npt),
        grid=(n // TN, npt + 1),
        in_specs=[prompt(k1), sample(k1), _wspec(w1, layer, k1, TN),
                  prompt(k2), sample(k2), _wspec(w2, layer, k2, TN),
                  pl.BlockSpec((TM, TN), lambda j, i: (i, j + ja)),
                  pl.BlockSpec((TM, TN), lambda j, i: (i, j + jb))],
        out_specs=pl.BlockSpec((TM, TN), lambda j, i: (i, j)),
        out_shape=jax.ShapeDtypeStruct((mp + TM, n), BF16),
        scratch_shapes=[pltpu.VMEM((k1, TN), BF16), pltpu.VMEM((k2, TN), BF16)],
        compiler_params=_cparams(*_MM_SEM),
        name="mm_merge",
    )(a1p, a1s, w1, a2p, a2s, w2, gates, gates)


def _ple_kernel(a_ref, wg_ref, p_ref, wp_ref, r_ref, o_ref, wsg_ref, wsp_ref):
    _cast_weights([(wg_ref, wsg_ref), (wp_ref, wsp_ref)])
    gate = _sigmoid(_dot(a_ref[...], wsg_ref[...]))
    o_ref[...] = r_ref[...] + gate * _dot(p_ref[...], wsp_ref[...])


def mm_ple(a, wg, p, wp, layer, res):
    m, k = a.shape
    kp = p.shape[1]
    n = wg.shape[-1]
    tm = TM_WIDE
    return pl.pallas_call(
        _ple_kernel,
        grid=(n // TN, m // tm),
        in_specs=[pl.BlockSpec((tm, k), lambda j, i: (i, 0)),
                  _wspec(wg, layer, k, TN),
                  pl.BlockSpec((tm, kp), lambda j, i: (i, 0)),
                  _wspec(wp, layer, kp, TN),
                  pl.BlockSpec((tm, TN), lambda j, i: (i, j))],
        out_specs=pl.BlockSpec((tm, TN), lambda j, i: (i, j)),
        out_shape=jax.ShapeDtypeStruct((m, n), F32),
        scratch_shapes=[pltpu.VMEM((k, TN), BF16), pltpu.VMEM((kp, TN), BF16)],
        compiler_params=_cparams(*_MM_SEM),
        name="mm_ple",
    )(a, wg, p, wp, res)


def _tri_cumsum(t, x):
    hi, mid, lo = _split3(x)
    return _dot(t, hi) + _dot(t, mid) + _dot(t, lo)


def _gates_kernel(x_ref, bias_ref, alog_ref, tseq_ref, tchk_ref,
                  dt_ref, lf_ref, c_ref, acs_ref, carry_ref, *, blocks_per_seq):
    i = pl.program_id(0)
    xb = x_ref[...] + bias_ref[...]
    t = jnp.log1p(jnp.exp(-jnp.abs(xb)))
    sp = jnp.maximum(xb, 0.0) + t
    ls = jnp.minimum(xb, 0.0) - t
    dt_ref[...] = sp
    lf_ref[...] = ls
    a = sp * (-jnp.exp(alog_ref[...]))
    acs_ref[...] = _tri_cumsum(tchk_ref[...], a)

    @pl.when(i % blocks_per_seq == 0)
    def _():
        carry_ref[...] = jnp.zeros_like(carry_ref)

    c = _tri_cumsum(tseq_ref[...], ls) + carry_ref[...]
    c_ref[...] = c
    carry_ref[...] = c[TM - 1:TM, :]


def gates(x, bias, alog, seq_len, chunk):
    m = x.shape[0]
    idx = np.arange(TM)
    low = idx[:, None] >= idx[None, :]
    tseq = jnp.asarray(low & (idx[:, None] // seq_len == idx[None, :] // seq_len), BF16)
    tchk = jnp.asarray(low & (idx[:, None] // chunk == idx[None, :] // chunk), BF16)
    row = pl.BlockSpec((TM, LANES), lambda i: (i, 0))
    vec = pl.BlockSpec((1, LANES), lambda i: (0, 0))
    tri = pl.BlockSpec((TM, TM), lambda i: (0, 0))
    return pl.pallas_call(
        functools.partial(_gates_kernel, blocks_per_seq=max(1, seq_len // TM)),
        grid=(m // TM,),
        in_specs=[row, vec, vec, tri, tri],
        out_specs=[row, row, row, row],
        out_shape=[jax.ShapeDtypeStruct((m, LANES), F32)] * 4,
        scratch_shapes=[pltpu.VMEM((1, LANES), F32)],
        compiler_params=_cparams("arbitrary"),
        name="gates",
    )(x, bias, alog, tseq, tchk)


def _attn_prompt_kernel(q_ref, k_ref, v_ref, ct_ref, o_ref):
    qi = pl.program_id(2)
    q0 = pl.multiple_of(qi * TQ, TQ)
    chains = [(g, rb) for g in range(KV_GROUP) for rb in range(TQ // RB)]
    crefs = [ct_ref[0, 0, g:g + 1, pl.ds(q0, LANES)][:, 0:1] for g in range(KV_GROUP)]

    def tile(g, rb, ks, nk, m, l, acc, masked):
        q = q_ref[rb * RB:(rb + 1) * RB, g * HEAD_DIM:(g + 1) * HEAD_DIM]
        kb = k_ref[pl.ds(ks, nk), :]
        vb = v_ref[pl.ds(ks, nk), :]
        s = _dot_nt(q, kb) + (crefs[g] - ct_ref[0, 0, g:g + 1, pl.ds(ks, nk)]) * LOG2E
        if masked:
            row = lax.broadcasted_iota(jnp.int32, (RB, nk), 0) + rb * RB
            col = lax.broadcasted_iota(jnp.int32, (RB, nk), 1)
            s = jnp.where(col <= row, s, NEG_INF)
        m_new = jnp.maximum(m, jnp.max(s, axis=-1, keepdims=True))
        p = jnp.exp2(s - m_new)
        alpha = jnp.exp2(m - m_new)
        l = alpha * l + jnp.sum(p, axis=-1, keepdims=True)
        acc = alpha * acc + _dot(p.astype(BF16), vb)
        return m_new, l, acc

    def body(ki, carry):
        ks = pl.multiple_of(ki * TQ, TQ)
        return tuple(tile(g, rb, ks, TQ, *carry[n], masked=False) for n, (g, rb) in enumerate(chains))

    init = tuple((jnp.full((RB, 1), NEG_INF, F32), jnp.zeros((RB, 1), F32), jnp.zeros((RB, HEAD_DIM), F32))
                 for _ in chains)
    carry = lax.fori_loop(0, qi, body, init)
    for n, (g, rb) in enumerate(chains):
        m, l, acc = tile(g, rb, q0, (rb + 1) * RB, *carry[n], masked=True)
        o_ref[rb * RB:(rb + 1) * RB, g * HEAD_DIM:(g + 1) * HEAD_DIM] = (acc / l).astype(o_ref.dtype)


def attn_prompt(q, k, v, ct, bsz, seq):
    nq = seq // TQ
    qw = KV_GROUP * HEAD_DIM
    return pl.pallas_call(
        _attn_prompt_kernel,
        grid=(bsz, N_KV_HEADS, nq),
        in_specs=[pl.BlockSpec((TQ, qw), lambda b, h, i: (b * nq + i, h)),
                  pl.BlockSpec((seq, HEAD_DIM), lambda b, h, i: (b, h)),
                  pl.BlockSpec((seq, HEAD_DIM), lambda b, h, i: (b, h)),
                  pl.BlockSpec((1, 1, SUBLANES, seq), lambda b, h, i: (b, h, 0, 0))],
        out_specs=pl.BlockSpec((TQ, qw), lambda b, h, i: (b * nq + i, h)),
        out_shape=jax.ShapeDtypeStruct((bsz * seq, N_HEADS * HEAD_DIM), BF16),
        compiler_params=_cparams("parallel", "parallel", "arbitrary"),
        name="attn_prompt",
    )(q, k, v, ct)


def _expand_heads(a):
    lane = lax.broadcasted_iota(jnp.int32, (a.shape[0], LANES), 1)
    blocks = [jnp.where(lane < SSM_HEAD_DIM, a[:, 2 * jj:2 * jj + 1], a[:, 2 * jj + 1:2 * jj + 2])
              for jj in range(HEADS_PER_GROUP // 2)]
    return jnp.concatenate(blocks, axis=1)


def _ssd_chunk(xs, bm, cm, z, dt8, acs8, acst, hprev, dsk, gs):
    q = xs.shape[0]
    acs8 = acs8 * LOG2E
    acst = acst * LOG2E
    dtx = _expand_heads(dt8)
    acsx = _expand_heads(acs8)
    acs_end = acsx[q - 1:q, :]
    xdt = xs * dtx
    cmb = cm.astype(BF16)
    bmb = bm.astype(BF16)
    cb = _dot_nt(cmb, bmb)
    row = lax.broadcasted_iota(jnp.int32, (q, q), 0)
    col = lax.broadcasted_iota(jnp.int32, (q, q), 1)
    causal = row >= col
    lane = lax.broadcasted_iota(jnp.int32, (q, LANES), 1)
    yblocks = []
    for jj in range(HEADS_PER_GROUP // 2):
        xblk = xdt[:, jj * LANES:(jj + 1) * LANES]
        acc = None
        for e in range(2):
            j = 2 * jj + e
            seg = acs8[:, j:j + 1] - acst[j:j + 1, :]
            lmat = jnp.exp2(jnp.where(causal, seg, NEG_INF))
            mh = (cb * lmat).astype(BF16)
            keep = (lane < SSM_HEAD_DIM) if e == 0 else (lane >= SSM_HEAD_DIM)
            part = _dot(mh, jnp.where(keep, xblk, 0.0).astype(BF16))
            acc = part if acc is None else acc + part
        yblocks.append(acc)
    y_diag = jnp.concatenate(yblocks, axis=1)
    y_off = _dot_nt(cmb, hprev.astype(BF16)) * jnp.exp2(acsx)
    xw = (xdt * jnp.exp2(acs_end - acsx)).astype(BF16)
    states = _dot_tn(xw, bmb)
    hnew = []
    for j in range(HEADS_PER_GROUP):
        sl = slice(j * SSM_HEAD_DIM, (j + 1) * SSM_HEAD_DIM)
        hnew.append(hprev[sl, :] * jnp.exp2(acst[j:j + 1, q - 1:q]) + states[sl, :])
    y = y_diag + y_off + dsk * xs
    y = y * _silu(z)
    ms = jnp.mean(y * y, axis=-1, keepdims=True)
    return y * lax.rsqrt(ms + EPS) * gs, hnew


def _conv_silu(cur_ref, buf_ref, w_ref, b_ref):
    rows = cur_ref.shape[0]
    buf_ref[SUBLANES:SUBLANES + rows, :] = cur_ref[...]
    out = b_ref[...]
    for j in range(CONV_W):
        lo = SUBLANES - (CONV_W - 1) + j
        out = out + buf_ref[lo:lo + rows, :] * w_ref[j:j + 1, :]
    buf_ref[0:SUBLANES, :] = buf_ref[rows:rows + SUBLANES, :]
    return _silu(out)


def _ssd_prompt_kernel(x_ref, b_ref, c_ref, z_ref, dt_ref, acs_ref, acst_ref,
                       wx_ref, wb_ref, wc_ref, bx_ref, bb_ref, bc_ref, dsk_ref, gs_ref,
                       o_ref, st_ref, h_ref, px_ref, pb_ref, pc_ref):
    ci = pl.program_id(2)

    @pl.when(ci == 0)
    def _():
        h_ref[...] = jnp.zeros_like(h_ref)
        for buf_ref in (px_ref, pb_ref, pc_ref):
            buf_ref[0:SUBLANES, :] = jnp.zeros((SUBLANES, buf_ref.shape[1]), F32)

    xs = _conv_silu(x_ref, px_ref, wx_ref, bx_ref)
    bm = _conv_silu(b_ref, pb_ref, wb_ref, bb_ref)
    cm = _conv_silu(c_ref, pc_ref, wc_ref, bc_ref)
    y, hnew = _ssd_chunk(xs, bm, cm, z_ref[...], dt_ref[0, 0], acs_ref[0, 0], acst_ref[0, 0],
                         h_ref[...], dsk_ref[...], gs_ref[...])
    for j, hj in enumerate(hnew):
        h_ref[j * SSM_HEAD_DIM:(j + 1) * SSM_HEAD_DIM, :] = hj
    o_ref[...] = y.astype(o_ref.dtype)

    @pl.when(ci == pl.num_programs(2) - 1)
    def _():
        st_ref[0] = h_ref[...]


def ssd_prompt(layer, zx, z_col, xbc_col, dtg, acsg, acstg, conv_w, conv_b, dskx, gs, bsz, seq):
    nc = seq // SSM_CHUNK
    d_inner = SSM_GROUPS * GROUP_WIDTH
    jz = z_col // GROUP_WIDTH
    jx = xbc_col // GROUP_WIDTH
    jb = (xbc_col + d_inner) // D_STATE
    jc = jb + SSM_GROUPS
    wjb = d_inner // D_STATE
    wjc = wjb + SSM_GROUPS
    q = SSM_CHUNK
    rowmap = lambda off: (lambda b, g, c: (b * nc + c, off + g))
    wmap = lambda off: (lambda b, g, c: (0, off + g))
    lmap = lambda off: (lambda b, g, c: (layer, 0, off + g))
    small = pl.BlockSpec((1, 1, q, HEADS_PER_GROUP), lambda b, g, c: (b, g, c, 0))
    in_specs = [
        pl.BlockSpec((q, GROUP_WIDTH), rowmap(jx)),
        pl.BlockSpec((q, D_STATE), rowmap(jb)),
        pl.BlockSpec((q, D_STATE), rowmap(jc)),
        pl.BlockSpec((q, GROUP_WIDTH), rowmap(jz)),
        small, small,
        pl.BlockSpec((1, 1, HEADS_PER_GROUP, q), lambda b, g, c: (b, g, 0, c)),
        pl.BlockSpec((None, CONV_W, GROUP_WIDTH), lmap(0)),
        pl.BlockSpec((None, CONV_W, D_STATE), lmap(wjb)),
        pl.BlockSpec((None, CONV_W, D_STATE), lmap(wjc)),
        pl.BlockSpec((None, 1, GROUP_WIDTH), lmap(0)),
        pl.BlockSpec((None, 1, D_STATE), lmap(wjb)),
        pl.BlockSpec((None, 1, D_STATE), lmap(wjc)),
        pl.BlockSpec((1, GROUP_WIDTH), wmap(0)),
        pl.BlockSpec((1, GROUP_WIDTH), wmap(0)),
    ]
    return pl.pallas_call(
        _ssd_prompt_kernel,
        grid=(bsz, SSM_GROUPS, nc),
        in_specs=in_specs,
        out_specs=[pl.BlockSpec((q, GROUP_WIDTH), lambda b, g, c: (b * nc + c, g)),
                   pl.BlockSpec((1, GROUP_WIDTH, D_STATE), lambda b, g, c: (b, g, 0))],
        out_shape=[jax.ShapeDtypeStruct((bsz * seq, d_inner), BF16),
                   jax.ShapeDtypeStruct((bsz, d_inner, D_STATE), F32)],
        scratch_shapes=[pltpu.VMEM((GROUP_WIDTH, D_STATE), F32),
                        pltpu.VMEM((SUBLANES + q, GROUP_WIDTH), F32),
                        pltpu.VMEM((SUBLANES + q, D_STATE), F32),
                        pltpu.VMEM((SUBLANES + q, D_STATE), F32)],
        compiler_params=_cparams("parallel", "parallel", "arbitrary"),
        name="ssd_prompt",
    )(zx, zx, zx, zx, dtg, acsg, acstg, conv_w, conv_w, conv_w, conv_b, conv_b, conv_b, dskx, gs)


def _ssd_sample_kernel(h_ref, xbc_ref, hist_ref, z_ref, dt_ref, acs_ref, acst_ref,
                       wc_ref, bc_ref, dsk_ref, gs_ref, *refs, n_new, aliased):
    if aliased:
        refs = refs[1:]
    o_ref, hn_ref, xp_ref, xa_ref, z16_ref = refs
    d_inner = SSM_GROUPS * GROUP_WIDTH
    xp_ref[...] = jnp.zeros_like(xp_ref)
    xp_ref[0:CONV_W - 1, :] = hist_ref[0]
    xp_ref[CONV_W - 1:CONV_W - 1 + n_new, :] = xbc_ref[0]
    xp = xp_ref[...]
    out = bc_ref[...]
    for j in range(CONV_W):
        out = out + xp[j:j + n_new] * wc_ref[j:j + 1, :]
    xa_ref[...] = jnp.zeros_like(xa_ref)
    xa_ref[0:n_new, :] = _silu(out)
    z16_ref[...] = jnp.zeros_like(z16_ref)
    z16_ref[0:n_new, :] = z_ref[0]
    dt16 = dt_ref[0]
    acs16 = acs_ref[0]
    acst = acst_ref[0]
    for g in range(SSM_GROUPS):
        cs = slice(g * GROUP_WIDTH, (g + 1) * GROUP_WIDTH)
        hs = slice(g * HEADS_PER_GROUP, (g + 1) * HEADS_PER_GROUP)
        bcol = d_inner + g * D_STATE
        ccol = d_inner + (SSM_GROUPS + g) * D_STATE
        y, hnew = _ssd_chunk(xa_ref[:, cs], xa_ref[:, bcol:bcol + D_STATE], xa_ref[:, ccol:ccol + D_STATE],
                             z16_ref[:, cs], dt16[:, hs], acs16[:, hs], acst[hs, :],
                             h_ref[0, cs, :], dsk_ref[:, cs], gs_ref[:, cs])
        for j, hj in enumerate(hnew):
            r0 = g * GROUP_WIDTH + j * SSM_HEAD_DIM
            hn_ref[0, 0, r0:r0 + SSM_HEAD_DIM, :] = hj
        o_ref[0, :, cs] = y


def ssd_sample(layer, depth, h0, xbc, hist, z, dt16, acs16, acst, conv_w, conv_b, dskx, gs, prev_states):
    _, db, hd, n = h0.shape
    n_new = xbc.shape[1]
    conv_dim = xbc.shape[2]
    d_inner = SSM_GROUPS * GROUP_WIDTH
    nh = d_inner // SSM_HEAD_DIM
    per_b = lambda *shape: pl.BlockSpec((1,) + shape, lambda b: (b,) + (0,) * len(shape))
    full = lambda *shape: pl.BlockSpec(shape, lambda b: (0,) * len(shape))
    aliased = prev_states is not None
    per_lb = lambda *shape: pl.BlockSpec((None, 1) + shape, lambda b: (layer, b) + (0,) * len(shape))
    per_l = lambda *shape: pl.BlockSpec((None,) + shape, lambda b: (layer,) + (0,) * len(shape))
    in_specs = [per_lb(hd, n), per_b(n_new, conv_dim), per_lb(CONV_W - 1, conv_dim), per_b(n_new, d_inner),
                per_b(TOK16, nh), per_b(TOK16, nh), per_b(nh, TOK16),
                per_l(CONV_W, conv_dim), per_l(1, conv_dim), full(1, d_inner), full(1, d_inner)]
    args = [h0, xbc, hist, z, dt16, acs16, acst, conv_w, conv_b, dskx, gs]
    if aliased:
        in_specs.append(pl.BlockSpec(memory_space=pl.ANY))
        args.append(prev_states)
    return pl.pallas_call(
        functools.partial(_ssd_sample_kernel, n_new=n_new, aliased=aliased),
        grid=(db,),
        in_specs=in_specs,
        out_specs=[per_b(TOK16, d_inner),
                   pl.BlockSpec((1, 1, hd, n), lambda b: (layer, b, 0, 0))],
        out_shape=[jax.ShapeDtypeStruct((db, TOK16, d_inner), F32),
                   jax.ShapeDtypeStruct((depth, db, hd, n), F32)],
        scratch_shapes=[pltpu.VMEM((SUBLANES, conv_dim), F32),
                        pltpu.VMEM((TOK16, conv_dim), F32),
                        pltpu.VMEM((TOK16, d_inner), F32)],
        input_output_aliases={len(args) - 1: 1} if aliased else {},
        compiler_params=_cparams("parallel"),
        name="ssd_sample",
    )(*args)


def _page_head(page_ref, hk):
    return page_ref[0, 0, pl.ds(hk, PAGE, stride=N_KV_HEADS), :]


def _attn_sample_kernel(pt_ref, q_ref, kn_ref, vn_ref, nb_ref, tm_ref, *refs, n_new, pps):
    del pt_ref
    k_refs, v_refs, lf_refs = refs[0:pps], refs[pps:2 * pps], refs[2 * pps:3 * pps]
    o_ref, qs_ref, knp_ref, m_ref, l_ref, acc_ref, carry_ref, r_ref = refs[3 * pps:]
    b = pl.program_id(0)
    p = pl.program_id(1)
    rows = KV_GROUP * n_new

    @pl.when((b == 0) & (p == 0))
    def _():
        knp_ref[...] = jnp.zeros_like(knp_ref)

    @pl.when(p == 0)
    def _():
        for head in range(N_HEADS):
            qs_ref[head * n_new:(head + 1) * n_new, :] = q_ref[0, :, head * HEAD_DIM:(head + 1) * HEAD_DIM]
        knp_ref[0:n_new, :] = kn_ref[0]
        m_ref[...] = jnp.full_like(m_ref, NEG_INF)
        l_ref[...] = jnp.zeros_like(l_ref)
        acc_ref[...] = jnp.zeros_like(acc_ref)
        carry_ref[...] = jnp.zeros_like(carry_ref)

    tm = tm_ref[...]
    qh = [qs_ref[hk * rows:(hk + 1) * rows, :].astype(BF16) for hk in range(N_KV_HEADS)]
    s_cols = []
    for u in range(pps):
        lft = lf_refs[u][0, 0]
        hi, mid, lo = _split3(lft)
        r16 = _dot(hi, tm) + _dot(mid, tm) + _dot(lo, tm) + carry_ref[...]
        carry_ref[...] = carry_ref[...] + jnp.sum(lft, axis=-1, keepdims=True)
        for head in range(N_HEADS):
            r_ref[head * n_new:(head + 1) * n_new, u * PAGE:(u + 1) * PAGE] = jnp.broadcast_to(
                r16[head:head + 1, :], (n_new, PAGE))
        s_cols.append(jnp.concatenate(
            [_dot_nt(qh[hk], _page_head(k_refs[u], hk).astype(BF16)) for hk in range(N_KV_HEADS)], axis=0))
    s = jnp.concatenate(s_cols, axis=1) + r_ref[...] * LOG2E
    m_old = m_ref[...]
    m_new = jnp.maximum(m_old, jnp.max(s, axis=-1, keepdims=True))
    pr = jnp.exp2(s - m_new)
    alpha = jnp.exp2(m_old - m_new)
    l_ref[...] = alpha * l_ref[...] + jnp.sum(pr, axis=-1, keepdims=True)
    m_ref[...] = m_new
    acc_old = acc_ref[...]
    for hk in range(N_KV_HEADS):
        rs = slice(hk * rows, (hk + 1) * rows)
        upd = alpha[rs, :] * acc_old[rs, :]
        for u in range(pps):
            upd = upd + _dot(pr[rs, u * PAGE:(u + 1) * PAGE].astype(BF16),
                             _page_head(v_refs[u], hk).astype(BF16))
        acc_ref[rs, :] = upd

    @pl.when(p == pl.num_programs(1) - 1)
    def _():
        sn = jnp.concatenate(
            [_dot_nt(qh[hk], knp_ref[:, hk * HEAD_DIM:(hk + 1) * HEAD_DIM].astype(BF16))
             for hk in range(N_KV_HEADS)], axis=0) + nb_ref[0] * LOG2E
        m_old = m_ref[...]
        m_new = jnp.maximum(m_old, jnp.max(sn, axis=-1, keepdims=True))
        pn = jnp.exp2(sn - m_new)
        alpha = jnp.exp2(m_old - m_new)
        l = alpha * l_ref[...] + jnp.sum(pn, axis=-1, keepdims=True)
        acc = alpha * acc_ref[...]
        vn = vn_ref[0]
        for hk in range(N_KV_HEADS):
            rs = slice(hk * rows, (hk + 1) * rows)
            upd = acc[rs, :]
            for j in range(n_new):
                upd = upd + pn[rs, j:j + 1] * vn[j:j + 1, hk * HEAD_DIM:(hk + 1) * HEAD_DIM]
            acc_ref[rs, :] = upd / l[rs, :]
        for head in range(N_HEADS):
            o_ref[0, :, head * HEAD_DIM:(head + 1) * HEAD_DIM] = acc_ref[head * n_new:(head + 1) * n_new, :]


def attn_sample(layer, page_table, q, kn, vn, nb, cache_k, cache_v, cache_lft):
    db, n_new, _ = q.shape
    n_pages = page_table.shape[1]
    pps = PAGES_PER_STEP
    kvw = N_KV_HEADS * HEAD_DIM
    n_rows = n_new * N_HEADS
    idx = np.arange(PAGE)
    tm = jnp.asarray(idx[:, None] > idx[None, :], BF16)
    pt_flat = page_table.reshape(-1)

    def page(u, ndim):
        def index_map(b, p, pt):
            return (layer, pt[b * n_pages + (n_pages - 1 - (p * pps + u))]) + (0,) * ndim
        return index_map

    per_b = lambda *shape: pl.BlockSpec((1,) + shape, lambda b, p, pt: (b,) + (0,) * len(shape))
    kv_spec = lambda u: pl.BlockSpec((1, 1, PAGE * N_KV_HEADS, HEAD_DIM), page(u, 2))
    lf_spec = lambda u: pl.BlockSpec((1, 1, N_HEADS, PAGE), page(u, 2))
    grid_spec = pltpu.PrefetchScalarGridSpec(
        num_scalar_prefetch=1,
        grid=(db, n_pages // pps),
        in_specs=[per_b(n_new, N_HEADS * HEAD_DIM), per_b(n_new, kvw), per_b(n_new, kvw),
                  per_b(n_rows, LANES),
                  pl.BlockSpec((PAGE, PAGE), lambda b, p, pt: (0, 0))]
                 + [kv_spec(u) for u in range(pps)] + [kv_spec(u) for u in range(pps)]
                 + [lf_spec(u) for u in range(pps)],
        out_specs=per_b(n_new, N_HEADS * HEAD_DIM),
        scratch_shapes=[pltpu.VMEM((n_rows, HEAD_DIM), F32),
                        pltpu.VMEM((PAGE, kvw), F32),
                        pltpu.VMEM((n_rows, 1), F32),
                        pltpu.VMEM((n_rows, 1), F32),
                        pltpu.VMEM((n_rows, HEAD_DIM), F32),
                        pltpu.VMEM((N_HEADS, 1), F32),
                        pltpu.VMEM((n_rows, pps * PAGE), F32)],
    )
    return pl.pallas_call(
        functools.partial(_attn_sample_kernel, n_new=n_new, pps=pps),
        grid_spec=grid_spec,
        out_shape=jax.ShapeDtypeStruct((db, n_new, N_HEADS * HEAD_DIM), F32),
        compiler_params=_cparams("arbitrary", "arbitrary"),
        name="attn_sample",
    )(pt_flat, q, kn, vn, nb, tm, *([cache_k] * pps), *([cache_v] * pps), *([cache_lft] * pps))


def kernel(x_prompt, x_sample, cache_k, cache_v, cache_logf, state_ssm, state_conv, page_table,
           p_prompt, p_sample, g_ffn1, w_ffn1_in, w_ffn1_out, g_mix, w_in, b_f, g_q, g_k,
           conv_w, conv_b, dt_bias, a_log, d_skip, g_ssm, w_branch_attn, w_branch_ssm, w_out,
           g_ffn2, w_ffn2_in, w_ffn2_out, g_ple, w_ple_gate, w_ple_proj):
    depth = w_in.shape[0]
    bp, seq, d_model = x_prompt.shape
    db, n_new, _ = x_sample.shape
    mp, ms = bp * seq, db * n_new
    attn_w = N_HEADS * HEAD_DIM
    kv_w = N_KV_HEADS * HEAD_DIM
    d_inner = SSM_GROUPS * GROUP_WIDTH
    conv_dim = d_inner + 2 * SSM_GROUPS * D_STATE
    ssm_heads = d_inner // SSM_HEAD_DIM
    assert KV_GROUP * n_new == SUBLANES and page_table.shape[1] % PAGES_PER_STEP == 0
    o_q, o_k, o_v = 0, attn_w, attn_w + kv_w
    o_f = o_v + kv_w
    o_z = o_f + N_HEADS
    o_x = o_z + d_inner
    o_dt = o_x + conv_dim
    o_ga = o_dt + ssm_heads
    z_col, xbc_col = 0, d_inner
    ga_col, gb_col = 0, d_model
    scale = HEAD_DIM ** -0.5

    n_pool = cache_k.shape[1]
    ck = cache_k.reshape(depth, n_pool, PAGE * N_KV_HEADS, HEAD_DIM)
    cv = cache_v.reshape(depth, n_pool, PAGE * N_KV_HEADS, HEAD_DIM)
    clft = jnp.swapaxes(cache_logf, 2, 3)

    state4 = state_ssm.reshape(depth, db, d_inner, D_STATE)
    conv_b3 = conv_b.reshape(depth, 1, conv_dim)

    h = jnp.concatenate([x_prompt.reshape(mp, d_model), x_sample.reshape(ms, d_model)], axis=0)
    outs = [[] for _ in range(9)]
    states_s = None
    wt = jnp.swapaxes(w_in, 1, 2)
    for i in range(depth):
        gate_bias = jnp.concatenate(
            [dt_bias[i], b_f[i], jnp.zeros((LANES - ssm_heads - N_HEADS,), F32)]).reshape(1, LANES)
        alog_row = jnp.concatenate([a_log[i], jnp.zeros((LANES - ssm_heads,), F32)]).reshape(1, LANES)
        dskx = jnp.repeat(d_skip[i], SSM_HEAD_DIM).reshape(1, d_inner)
        gs = g_ssm[i].reshape(1, d_inner)

        act = mm_swiglu(rmsnorm(h, g_ffn1[i]), w_ffn1_in, i)
        h = mm_residual(act, w_ffn1_out, i, h, 0.5, TM)

        u = rmsnorm(h, g_mix[i])
        (q_b,) = mm_headnorm(u, wt, i, g_q[i], [BF16], o_q, attn_w, mult=scale * LOG2E)
        k_f, k_b = mm_headnorm(u, wt, i, g_k[i], [F32, BF16], o_k, kv_w)
        v_f, v_b = mm_plain(u, wt, i, [F32, BF16], TN, o_v, kv_w)
        (wide,) = mm_plain(u, wt, i, [F32], 2 * TN, o_z, d_inner + conv_dim)
        (gate_pre,) = mm_plain(u, wt, i, [F32], 2 * TN, o_ga, 2 * d_model)
        small = mm_gate_logits(u, wt, i, o_dt, ssm_heads, o_f, N_HEADS)

        dt_p, lf_p, c_p, acs_p = gates(small[:mp], gate_bias, alog_row, seq, SSM_CHUNK)
        dt_s, lf_s, c_s, acs_s = gates(small[mp:], gate_bias, alog_row, n_new, n_new)
        fcols = slice(ssm_heads, ssm_heads + N_HEADS)

        ct = c_p[:, fcols].reshape(bp, seq, N_KV_HEADS, KV_GROUP).transpose(0, 2, 3, 1)
        ct = jnp.pad(ct, ((0, 0), (0, 0), (0, SUBLANES - KV_GROUP), (0, 0)))
        o_attn_p = attn_prompt(q_b, k_b, v_b, ct, bp, seq)
        pg = lambda a: a[:, :ssm_heads].reshape(bp, seq, SSM_GROUPS, HEADS_PER_GROUP).transpose(0, 2, 1, 3)
        acsg = pg(acs_p)
        o_ssm_p, st_p = ssd_prompt(i, wide, z_col, xbc_col, pg(dt_p), acsg, acsg.transpose(0, 1, 3, 2),
                                   conv_w, conv_b3, dskx, gs, bp, seq)

        wide_s = wide[mp:].reshape(db, n_new, -1)
        xbc_s = wide_s[:, :, xbc_col:xbc_col + conv_dim]
        tail = TOK16 - n_new
        dt16 = jnp.pad(dt_s[:, :ssm_heads].reshape(db, n_new, ssm_heads), ((0, 0), (0, tail), (0, 0)))
        acs16 = jnp.pad(acs_s[:, :ssm_heads].reshape(db, n_new, ssm_heads), ((0, 0), (0, tail), (0, 0)),
                        mode="edge")
        o_ssm_s16, states_s = ssd_sample(
            i, depth, state4, xbc_s, state_conv, wide_s[:, :, z_col:z_col + d_inner],
            dt16, acs16, acs16.transpose(0, 2, 1), conv_w, conv_b3, dskx, gs, states_s)
        o_ssm_s = o_ssm_s16[:, :n_new].reshape(ms, d_inner).astype(BF16)

        cn = c_s[:, fcols].reshape(db, n_new, N_HEADS)
        tt = np.arange(n_new)
        nbias = jnp.where((tt[None, :] <= tt[:, None])[None, None, :, :],
                          -cn.transpose(0, 2, 1)[:, :, None, :], NEG_INF)
        nbias = jnp.pad(nbias.reshape(db, N_HEADS * n_new, n_new),
                        ((0, 0), (0, 0), (0, LANES - n_new)), constant_values=NEG_INF)
        o_attn_s = attn_sample(i, page_table,
                               q_b[mp:].astype(F32).reshape(db, n_new, attn_w),
                               k_b[mp:].astype(F32).reshape(db, n_new, kv_w),
                               v_f[mp:].reshape(db, n_new, kv_w),
                               nbias, ck, cv, clft)
        o_attn_s = o_attn_s.reshape(ms, attn_w).astype(BF16)

        merged = mm_merge(o_attn_p, o_attn_s, w_branch_attn, o_ssm_p, o_ssm_s, w_branch_ssm, i,
                          gate_pre, ga_col, gb_col)
        h = mm_residual(merged, w_out, i, h, 1.0, TM_WIDE)

        act = mm_swiglu(rmsnorm(h, g_ffn2[i]), w_ffn2_in, i)
        h = mm_residual(act, w_ffn2_out, i, h, 0.5, TM)

        p_l = jnp.concatenate([p_prompt[i].reshape(mp, -1), p_sample[i].reshape(ms, -1)], axis=0).astype(BF16)
        h = mm_ple(rmsnorm(h, g_ple[i]), w_ple_gate, p_l, w_ple_proj, i, h)

        conv_p = jnp.stack([wide[(b + 1) * seq - (CONV_W - 1):(b + 1) * seq, xbc_col:xbc_col + conv_dim]
                            for b in range(bp)])
        layer_out = (
            k_f[:mp].reshape(bp, seq, N_KV_HEADS, HEAD_DIM), v_f[:mp].reshape(bp, seq, N_KV_HEADS, HEAD_DIM),
            lf_p[:, fcols].reshape(bp, seq, N_HEADS), conv_p,
            st_p.reshape(bp, ssm_heads, SSM_HEAD_DIM, D_STATE),
            k_f[mp:].reshape(db, n_new, N_KV_HEADS, HEAD_DIM), v_f[mp:].reshape(db, n_new, N_KV_HEADS, HEAD_DIM),
            lf_s[:, fcols].reshape(db, n_new, N_HEADS),
            jnp.concatenate([state_conv[i], xbc_s], axis=1)[:, n_new:n_new + CONV_W - 1])
        for lst, val in zip(outs, layer_out):
            lst.append(val)

    stacked = [jnp.stack(lst) for lst in outs]
    return (h[:mp].reshape(bp, seq, d_model), h[mp:].reshape(db, n_new, d_model), *stacked,
            states_s.reshape(depth, db, ssm_heads, SSM_HEAD_DIM, D_STATE))
```

```python
import functools

import numpy as np
import jax
import jax.numpy as jnp
from jax import lax
from jax.experimental import pallas as pl
from jax.experimental.pallas import tpu as pltpu

F32 = jnp.float32
BF16 = jnp.bfloat16
EPS = 1e-6
NEG_INF = float("-inf")

LANES = 128
SUBLANES = 8
VMEM_LIMIT_BYTES = 56 * 1024 * 1024

HEAD_DIM = 128
N_HEADS = 16
N_KV_HEADS = 8
KV_GROUP = N_HEADS // N_KV_HEADS
SSM_HEAD_DIM = 64
SSM_GROUPS = 8
HEADS_PER_GROUP = 8
GROUP_WIDTH = HEADS_PER_GROUP * SSM_HEAD_DIM
D_STATE = 128
CONV_W = 4
SSM_CHUNK = 128
PAGE = 128

TM = 512
TM_WIDE = 1088
TN = 512
TQ = 1024
LOG2E = 1.4426950408889634
TOK16 = 16
PAGES_PER_STEP = 8


def _cparams(*sem):
    return pltpu.CompilerParams(dimension_semantics=sem, vmem_limit_bytes=VMEM_LIMIT_BYTES)


def _sigmoid(x):
    return 0.5 + 0.5 * jnp.tanh(0.5 * x)


def _silu(x):
    return x * _sigmoid(x)


def _dot(a, b):
    return jnp.dot(a, b, preferred_element_type=F32)


def _dot_nt(a, b):
    return lax.dot_general(a, b, (((1,), (1,)), ((), ())), preferred_element_type=F32)


def _dot_tn(a, b):
    return lax.dot_general(a, b, (((0,), (0,)), ((), ())), preferred_element_type=F32)


def _split3(x):
    hi = x.astype(BF16)
    r1 = x - hi.astype(F32)
    mid = r1.astype(BF16)
    lo = (r1 - mid.astype(F32)).astype(BF16)
    return hi, mid, lo


def _rmsnorm_kernel(x_ref, g_ref, o_ref):
    x = x_ref[...]
    ms = jnp.mean(x * x, axis=-1, keepdims=True)
    o_ref[...] = (x * lax.rsqrt(ms + EPS) * g_ref[...]).astype(o_ref.dtype)


def rmsnorm(x, g):
    m, d = x.shape
    return pl.pallas_call(
        _rmsnorm_kernel,
        grid=(m // TM,),
        in_specs=[pl.BlockSpec((TM, d), lambda i: (i, 0)),
                  pl.BlockSpec((1, d), lambda i: (0, 0))],
        out_specs=pl.BlockSpec((TM, d), lambda i: (i, 0)),
        out_shape=jax.ShapeDtypeStruct((m, d), BF16),
        compiler_params=_cparams("parallel"),
        name="rmsnorm",
    )(x, g.reshape(1, d))


def _cast_weights(pairs):
    @pl.when(pl.program_id(1) == 0)
    def _():
        for w_ref, ws_ref in pairs:
            ws_ref[...] = w_ref[...].reshape(ws_ref.shape).astype(BF16)


_MM_SEM = ("parallel", "arbitrary")


def _wspec(w, layer, k, tn, j0=0):
    if w.ndim == 2:
        return pl.BlockSpec((k, tn), lambda j, i: (0, j + j0))
    return pl.BlockSpec((None, k, tn), lambda j, i: (layer, 0, j + j0))


def _swiglu_kernel(a_ref, wa_ref, wb_ref, o_ref, wsa_ref, wsb_ref):
    _cast_weights([(wa_ref, wsa_ref), (wb_ref, wsb_ref)])
    a = a_ref[...]
    o_ref[...] = (_silu(_dot(a, wsa_ref[...])) * _dot(a, wsb_ref[...])).astype(o_ref.dtype)


def mm_swiglu(a, w, layer):
    m, k = a.shape
    dff = w.shape[-1] // 2
    nj = dff // TN
    tm = TM_WIDE
    return pl.pallas_call(
        _swiglu_kernel,
        grid=(nj, m // tm),
        in_specs=[pl.BlockSpec((tm, k), lambda j, i: (i, 0)),
                  _wspec(w, layer, k, TN), _wspec(w, layer, k, TN, nj)],
        out_specs=pl.BlockSpec((tm, TN), lambda j, i: (i, j)),
        out_shape=jax.ShapeDtypeStruct((m, dff), BF16),
        scratch_shapes=[pltpu.VMEM((k, TN), BF16), pltpu.VMEM((k, TN), BF16)],
        compiler_params=_cparams(*_MM_SEM),
        name="mm_swiglu",
    )(a, w, w)


def _residual_kernel(a_ref, w_ref, r_ref, o_ref, ws_ref, *, alpha):
    _cast_weights([(w_ref, ws_ref)])
    o_ref[...] = r_ref[...] + alpha * _dot(a_ref[...], ws_ref[...])


def mm_residual(a, w, layer, res, alpha, tm):
    m, k = a.shape
    n = w.shape[-1]
    return pl.pallas_call(
        functools.partial(_residual_kernel, alpha=alpha),
        grid=(n // TN, m // tm),
        in_specs=[pl.BlockSpec((tm, k), lambda j, i: (i, 0)),
                  _wspec(w, layer, k, TN),
                  pl.BlockSpec((tm, TN), lambda j, i: (i, j))],
        out_specs=pl.BlockSpec((tm, TN), lambda j, i: (i, j)),
        out_shape=jax.ShapeDtypeStruct((m, n), F32),
        scratch_shapes=[pltpu.VMEM((k, TN), BF16)],
        compiler_params=_cparams(*_MM_SEM),
        name="mm_residual",
    )(a, w, res)


def _wtspec(layer, k, rows, row0, stride=0):
    return pl.BlockSpec((pl.Element(1), pl.Element(rows), pl.Element(k)),
                        lambda j, i: (layer, pl.multiple_of(row0 + j * stride, SUBLANES), 0))


def _plain_kernel(a_ref, w_ref, *refs):
    *o_refs, ws_ref = refs
    _cast_weights([(w_ref, ws_ref)])
    acc = _dot_nt(a_ref[...], ws_ref[...])
    for o_ref in o_refs:
        o_ref[...] = acc.astype(o_ref.dtype)


def mm_plain(a, wt, layer, out_dtypes, tn, row0, n):
    m, k = a.shape
    tm = TM_WIDE
    return pl.pallas_call(
        _plain_kernel,
        grid=(n // tn, m // tm),
        in_specs=[pl.BlockSpec((tm, k), lambda j, i: (i, 0)),
                  _wtspec(layer, k, tn, row0, tn)],
        out_specs=[pl.BlockSpec((tm, tn), lambda j, i: (i, j)) for _ in out_dtypes],
        out_shape=[jax.ShapeDtypeStruct((m, n), dt) for dt in out_dtypes],
        scratch_shapes=[pltpu.VMEM((tn, k), BF16)],
        compiler_params=_cparams(*_MM_SEM),
        name="mm_plain",
    )(a, wt)


def _gate_logits_kernel(a_ref, wdt_ref, wf_ref, o_ref, ws_ref):
    @pl.when(pl.program_id(1) == 0)
    def _():
        ndt, nf = wdt_ref.shape[1], wf_ref.shape[1]
        ws_ref[...] = jnp.zeros_like(ws_ref)
        ws_ref[0:ndt, :] = wdt_ref[0].astype(BF16)
        ws_ref[ndt:ndt + nf, :] = wf_ref[0].astype(BF16)

    o_ref[...] = _dot_nt(a_ref[...], ws_ref[...])


def mm_gate_logits(a, wt, layer, dt_row0, n_dt, f_row0, n_f):
    m, k = a.shape
    tm = TM_WIDE
    return pl.pallas_call(
        _gate_logits_kernel,
        grid=(1, m // tm),
        in_specs=[pl.BlockSpec((tm, k), lambda j, i: (i, 0)),
                  _wtspec(layer, k, n_dt, dt_row0), _wtspec(layer, k, n_f, f_row0)],
        out_specs=pl.BlockSpec((tm, LANES), lambda j, i: (i, 0)),
        out_shape=jax.ShapeDtypeStruct((m, LANES), F32),
        scratch_shapes=[pltpu.VMEM((LANES, k), BF16)],
        compiler_params=_cparams(*_MM_SEM),
        name="mm_gate_logits",
    )(a, wt, wt)


def _headnorm_kernel(a_ref, w_ref, g_ref, *refs, mult):
    *o_refs, ws_ref = refs
    _cast_weights([(w_ref, ws_ref)])
    acc = _dot_nt(a_ref[...], ws_ref[...])
    g = g_ref[...]
    for hh in range(acc.shape[1] // HEAD_DIM):
        sl = slice(hh * HEAD_DIM, (hh + 1) * HEAD_DIM)
        blk = acc[:, sl]
        ms = jnp.mean(blk * blk, axis=-1, keepdims=True)
        y = blk * lax.rsqrt(ms + EPS) * g
        for o_ref in o_refs:
            o_ref[:, sl] = (y * mult).astype(o_ref.dtype) if o_ref.dtype == BF16 else y


def mm_headnorm(a, wt, layer, g, out_dtypes, row0, n, mult=1.0):
    m, k = a.shape
    tm = TM_WIDE
    return pl.pallas_call(
        functools.partial(_headnorm_kernel, mult=mult),
        grid=(n // TN, m // tm),
        in_specs=[pl.BlockSpec((tm, k), lambda j, i: (i, 0)),
                  _wtspec(layer, k, TN, row0, TN),
                  pl.BlockSpec((1, HEAD_DIM), lambda j, i: (0, 0))],
        out_specs=[pl.BlockSpec((tm, TN), lambda j, i: (i, j)) for _ in out_dtypes],
        out_shape=[jax.ShapeDtypeStruct((m, n), dt) for dt in out_dtypes],
        scratch_shapes=[pltpu.VMEM((TN, k), BF16)],
        compiler_params=_cparams(*_MM_SEM),
        name="mm_headnorm",
    )(a, wt, g.reshape(1, HEAD_DIM))


def _merge_kernel(a1p_ref, a1s_ref, w1_ref, a2p_ref, a2s_ref, w2_ref, ga_ref, gb_ref, o_ref, ws1_ref, ws2_ref,
                  *, prompt_tiles):
    _cast_weights([(w1_ref, ws1_ref), (w2_ref, ws2_ref)])

    def emit(a1_ref, a2_ref):
        o_ref[...] = (_sigmoid(ga_ref[...]) * _dot(a1_ref[...], ws1_ref[...])
                      + _sigmoid(gb_ref[...]) * _dot(a2_ref[...], ws2_ref[...])).astype(o_ref.dtype)

    @pl.when(pl.program_id(1) < prompt_tiles)
    def _():
        emit(a1p_ref, a2p_ref)

    @pl.when(pl.program_id(1) >= prompt_tiles)
    def _():
        emit(a1s_ref, a2s_ref)


def mm_merge(a1p, a1s, w1, a2p, a2s, w2, layer, gates, ga_col, gb_col):
    mp, k1 = a1p.shape
    k2 = a2p.shape[1]
    n = w1.shape[-1]
    assert a1s.shape[0] == TM and a2s.shape[0] == TM
    npt = mp // TM
    ja, jb = ga_col // TN, gb_col // TN
    prompt = lambda k: pl.BlockSpec((TM, k), lambda j, i: (jnp.minimum(i, npt - 1), 0))
    sample = lambda k: pl.BlockSpec((TM, k), lambda j, i: (0, 0))
    return pl.pallas_call(
        functools.partial(_merge_kernel, prompt_tiles=npt),
        grid=(n // TN, npt + 1),
        in_specs=[prompt(k1), sample(k1), _wspec(w1, layer, k1, TN),
                  prompt(k2), sample(k2), _wspec(w2, layer, k2, TN),
                  pl.BlockSpec((TM, TN), lambda j, i: (i, j + ja)),
                  pl.BlockSpec((TM, TN), lambda j, i: (i, j + jb))],
        out_specs=pl.BlockSpec((TM, TN), lambda j, i: (i, j)),
        out_shape=jax.ShapeDtypeStruct((mp + TM, n), BF16),
        scratch_shapes=[pltpu.VMEM((k1, TN), BF16), pltpu.VMEM((k2, TN), BF16)],
        compiler_params=_cparams(*_MM_SEM),
        name="mm_merge",
    )(a1p, a1s, w1, a2p, a2s, w2, gates, gates)


def _ple_kernel(a_ref, wg_ref, p_ref, wp_ref, r_ref, o_ref, wsg_ref, wsp_ref):
    _cast_weights([(wg_ref, wsg_ref), (wp_ref, wsp_ref)])
    gate = _sigmoid(_dot(a_ref[...], wsg_ref[...]))
    o_ref[...] = r_ref[...] + gate * _dot(p_ref[...], wsp_ref[...])


def mm_ple(a, wg, p, wp, layer, res):
    m, k = a.shape
    kp = p.shape[1]
    n = wg.shape[-1]
    tm = TM_WIDE
    return pl.pallas_call(
        _ple_kernel,
        grid=(n // TN, m // tm),
        in_specs=[pl.BlockSpec((tm, k), lambda j, i: (i, 0)),
                  _wspec(wg, layer, k, TN),
                  pl.BlockSpec((tm, kp), lambda j, i: (i, 0)),
                  _wspec(wp, layer, kp, TN),
                  pl.BlockSpec((tm, TN), lambda j, i: (i, j))],
        out_specs=pl.BlockSpec((tm, TN), lambda j, i: (i, j)),
        out_shape=jax.ShapeDtypeStruct((m, n), F32),
        scratch_shapes=[pltpu.VMEM((k, TN), BF16), pltpu.VMEM((kp, TN), BF16)],
        compiler_params=_cparams(*_MM_SEM),
        name="mm_ple",
    )(a, wg, p, wp, res)


def _tri_cumsum(t, x):
    hi, mid, lo = _split3(x)
    return _dot(t, hi) + _dot(t, mid) + _dot(t, lo)


def _gates_kernel(x_ref, bias_ref, alog_ref, tseq_ref, tchk_ref,
                  dt_ref, lf_ref, c_ref, acs_ref, carry_ref, *, blocks_per_seq):
    i = pl.program_id(0)
    xb = x_ref[...] + bias_ref[...]
    t = jnp.log1p(jnp.exp(-jnp.abs(xb)))
    sp = jnp.maximum(xb, 0.0) + t
    ls = jnp.minimum(xb, 0.0) - t
    dt_ref[...] = sp
    lf_ref[...] = ls
    a = sp * (-jnp.exp(alog_ref[...]))
    acs_ref[...] = _tri_cumsum(tchk_ref[...], a)

    @pl.when(i % blocks_per_seq == 0)
    def _():
        carry_ref[...] = jnp.zeros_like(carry_ref)

    c = _tri_cumsum(tseq_ref[...], ls) + carry_ref[...]
    c_ref[...] = c
    carry_ref[...] = c[TM - 1:TM, :]


def gates(x, bias, alog, seq_len, chunk):
    m = x.shape[0]
    idx = np.arange(TM)
    low = idx[:, None] >= idx[None, :]
    tseq = jnp.asarray(low & (idx[:, None] // seq_len == idx[None, :] // seq_len), BF16)
    tchk = jnp.asarray(low & (idx[:, None] // chunk == idx[None, :] // chunk), BF16)
    row = pl.BlockSpec((TM, LANES), lambda i: (i, 0))
    vec = pl.BlockSpec((1, LANES), lambda i: (0, 0))
    tri = pl.BlockSpec((TM, TM), lambda i: (0, 0))
    return pl.pallas_call(
        functools.partial(_gates_kernel, blocks_per_seq=max(1, seq_len // TM)),
        grid=(m // TM,),
        in_specs=[row, vec, vec, tri, tri],
        out_specs=[row, row, row, row],
        out_shape=[jax.ShapeDtypeStruct((m, LANES), F32)] * 4,
        scratch_shapes=[pltpu.VMEM((1, LANES), F32)],
        compiler_params=_cparams("arbitrary"),
        name="gates",
    )(x, bias, alog, tseq, tchk)


def _attn_prompt_kernel(q_ref, k_ref, v_ref, ct_ref, o_ref):
    qi = pl.program_id(2)
    q0 = pl.multiple_of(qi * TQ, TQ)
    crefs = [ct_ref[0, 0, g:g + 1, pl.ds(q0, LANES)][:, 0:1] for g in range(KV_GROUP)]
    half = TQ // 2

    def tile(g, r0, nr, ks, nk, m, l, acc, diag_off=None):
        q = q_ref[r0:r0 + nr, g * HEAD_DIM:(g + 1) * HEAD_DIM]
        kb = k_ref[pl.ds(ks, nk), :]
        vb = v_ref[pl.ds(ks, nk), :]
        s = _dot_nt(q, kb) + (crefs[g] - ct_ref[0, 0, g:g + 1, pl.ds(ks, nk)]) * LOG2E
        if diag_off is not None:
            row = lax.broadcasted_iota(jnp.int32, (nr, nk), 0) + r0
            col = lax.broadcasted_iota(jnp.int32, (nr, nk), 1) + diag_off
            s = jnp.where(col <= row, s, NEG_INF)
        m_new = jnp.maximum(m, jnp.max(s, axis=-1, keepdims=True))
        p = jnp.exp2(s - m_new)
        alpha = jnp.exp2(m - m_new)
        l = alpha * l + jnp.sum(p, axis=-1, keepdims=True)
        acc = alpha * acc + _dot(p.astype(BF16), vb)
        return m_new, l, acc

    def body(ki, carry):
        ks = pl.multiple_of(ki * TQ, TQ)
        return tuple(tile(g, 0, TQ, ks, TQ, *carry[g]) for g in range(KV_GROUP))

    init = tuple((jnp.full((TQ, 1), NEG_INF, F32), jnp.zeros((TQ, 1), F32), jnp.zeros((TQ, HEAD_DIM), F32))
                 for _ in range(KV_GROUP))
    carry = lax.fori_loop(0, qi, body, init)
    first = [tile(g, 0, TQ, q0, half, *carry[g], diag_off=0) for g in range(KV_GROUP)]
    for g in range(KV_GROUP):
        m, l, acc = first[g]
        hs = slice(g * HEAD_DIM, (g + 1) * HEAD_DIM)
        o_ref[0:half, hs] = (acc[0:half] / l[0:half]).astype(o_ref.dtype)
        m, l, acc = tile(g, half, half, pl.multiple_of(q0 + half, half), half,
                         m[half:], l[half:], acc[half:], diag_off=half)
        o_ref[half:TQ, hs] = (acc / l).astype(o_ref.dtype)


def attn_prompt(q, k, v, ct, bsz, seq):
    nq = seq // TQ
    qw = KV_GROUP * HEAD_DIM
    return pl.pallas_call(
        _attn_prompt_kernel,
        grid=(bsz, N_KV_HEADS, nq),
        in_specs=[pl.BlockSpec((TQ, qw), lambda b, h, i: (b * nq + i, h)),
                  pl.BlockSpec((seq, HEAD_DIM), lambda b, h, i: (b, h)),
                  pl.BlockSpec((seq, HEAD_DIM), lambda b, h, i: (b, h)),
                  pl.BlockSpec((1, 1, SUBLANES, seq), lambda b, h, i: (b, h, 0, 0))],
        out_specs=pl.BlockSpec((TQ, qw), lambda b, h, i: (b * nq + i, h)),
        out_shape=jax.ShapeDtypeStruct((bsz * seq, N_HEADS * HEAD_DIM), BF16),
        compiler_params=_cparams("parallel", "parallel", "arbitrary"),
        name="attn_prompt",
    )(q, k, v, ct)


def _expand_heads(a):
    lane = lax.broadcasted_iota(jnp.int32, (a.shape[0], LANES), 1)
    blocks = [jnp.where(lane < SSM_HEAD_DIM, a[:, 2 * jj:2 * jj + 1], a[:, 2 * jj + 1:2 * jj + 2])
              for jj in range(HEADS_PER_GROUP // 2)]
    return jnp.concatenate(blocks, axis=1)


def _ssd_chunk(xs, bm, cm, z, dt8, acs8, acst, hprev, dsk, gs):
    q = xs.shape[0]
    acs8 = acs8 * LOG2E
    acst = acst * LOG2E
    dtx = _expand_heads(dt8)
    acsx = _expand_heads(acs8)
    acs_end = acsx[q - 1:q, :]
    xdt = xs * dtx
    cmb = cm.astype(BF16)
    bmb = bm.astype(BF16)
    cb = _dot_nt(cmb, bmb)
    row = lax.broadcasted_iota(jnp.int32, (q, q), 0)
    col = lax.broadcasted_iota(jnp.int32, (q, q), 1)
    causal = row >= col
    lane = lax.broadcasted_iota(jnp.int32, (q, LANES), 1)
    yblocks = []
    for jj in range(HEADS_PER_GROUP // 2):
        xblk = xdt[:, jj * LANES:(jj + 1) * LANES]
        acc = None
        for e in range(2):
            j = 2 * jj + e
            seg = acs8[:, j:j + 1] - acst[j:j + 1, :]
            lmat = jnp.exp2(jnp.where(causal, seg, NEG_INF))
            mh = (cb * lmat).astype(BF16)
            keep = (lane < SSM_HEAD_DIM) if e == 0 else (lane >= SSM_HEAD_DIM)
            part = _dot(mh, jnp.where(keep, xblk, 0.0).astype(BF16))
            acc = part if acc is None else acc + part
        yblocks.append(acc)
    y_diag = jnp.concatenate(yblocks, axis=1)
    y_off = _dot_nt(cmb, hprev.astype(BF16)) * jnp.exp2(acsx)
    xw = (xdt * jnp.exp2(acs_end - acsx)).astype(BF16)
    states = _dot_tn(xw, bmb)
    hnew = []
    for j in range(HEADS_PER_GROUP):
        sl = slice(j * SSM_HEAD_DIM, (j + 1) * SSM_HEAD_DIM)
        hnew.append(hprev[sl, :] * jnp.exp2(acst[j:j + 1, q - 1:q]) + states[sl, :])
    y = y_diag + y_off + dsk * xs
    y = y * _silu(z)
    ms = jnp.mean(y * y, axis=-1, keepdims=True)
    return y * lax.rsqrt(ms + EPS) * gs, hnew


def _conv_silu(cur_ref, buf_ref, w_ref, b_ref):
    rows = cur_ref.shape[0]
    cur = cur_ref[...]
    full = jnp.concatenate([buf_ref[0:SUBLANES, :], cur], axis=0)
    out = b_ref[...]
    for j in range(CONV_W - 1):
        out = out + pltpu.roll(full, CONV_W - 1 - j, 0)[SUBLANES:SUBLANES + rows] * w_ref[j:j + 1, :]
    out = out + cur * w_ref[CONV_W - 1:CONV_W, :]
    buf_ref[0:SUBLANES, :] = cur[rows - SUBLANES:rows]
    return _silu(out)


def _ssd_prompt_kernel(x_ref, b_ref, c_ref, z_ref, dt_ref, acs_ref, acst_ref,
                       wx_ref, wb_ref, wc_ref, bx_ref, bb_ref, bc_ref, dsk_ref, gs_ref,
                       o_ref, st_ref, h_ref, px_ref, pb_ref, pc_ref):
    ci = pl.program_id(2)

    @pl.when(ci == 0)
    def _():
        h_ref[...] = jnp.zeros_like(h_ref)
        for buf_ref in (px_ref, pb_ref, pc_ref):
            buf_ref[0:SUBLANES, :] = jnp.zeros((SUBLANES, buf_ref.shape[1]), F32)

    xs = _conv_silu(x_ref, px_ref, wx_ref, bx_ref)
    bm = _conv_silu(b_ref, pb_ref, wb_ref, bb_ref)
    cm = _conv_silu(c_ref, pc_ref, wc_ref, bc_ref)
    y, hnew = _ssd_chunk(xs, bm, cm, z_ref[...], dt_ref[0, 0], acs_ref[0, 0], acst_ref[0, 0],
                         h_ref[...], dsk_ref[...], gs_ref[...])
    for j, hj in enumerate(hnew):
        h_ref[j * SSM_HEAD_DIM:(j + 1) * SSM_HEAD_DIM, :] = hj
    o_ref[...] = y.astype(o_ref.dtype)

    @pl.when(ci == pl.num_programs(2) - 1)
    def _():
        st_ref[0] = h_ref[...]


def ssd_prompt(layer, zx, z_col, xbc_col, dtg, acsg, acstg, conv_w, conv_b, dskx, gs, bsz, seq):
    nc = seq // SSM_CHUNK
    d_inner = SSM_GROUPS * GROUP_WIDTH
    jz = z_col // GROUP_WIDTH
    jx = xbc_col // GROUP_WIDTH
    jb = (xbc_col + d_inner) // D_STATE
    jc = jb + SSM_GROUPS
    wjb = d_inner // D_STATE
    wjc = wjb + SSM_GROUPS
    q = SSM_CHUNK
    rowmap = lambda off: (lambda b, g, c: (b * nc + c, off + g))
    wmap = lambda off: (lambda b, g, c: (0, off + g))
    lmap = lambda off: (lambda b, g, c: (layer, 0, off + g))
    small = pl.BlockSpec((1, 1, q, HEADS_PER_GROUP), lambda b, g, c: (b, g, c, 0))
    in_specs = [
        pl.BlockSpec((q, GROUP_WIDTH), rowmap(jx)),
        pl.BlockSpec((q, D_STATE), rowmap(jb)),
        pl.BlockSpec((q, D_STATE), rowmap(jc)),
        pl.BlockSpec((q, GROUP_WIDTH), rowmap(jz)),
        small, small,
        pl.BlockSpec((1, 1, HEADS_PER_GROUP, q), lambda b, g, c: (b, g, 0, c)),
        pl.BlockSpec((None, CONV_W, GROUP_WIDTH), lmap(0)),
        pl.BlockSpec((None, CONV_W, D_STATE), lmap(wjb)),
        pl.BlockSpec((None, CONV_W, D_STATE), lmap(wjc)),
        pl.BlockSpec((None, 1, GROUP_WIDTH), lmap(0)),
        pl.BlockSpec((None, 1, D_STATE), lmap(wjb)),
        pl.BlockSpec((None, 1, D_STATE), lmap(wjc)),
        pl.BlockSpec((1, GROUP_WIDTH), wmap(0)),
        pl.BlockSpec((1, GROUP_WIDTH), wmap(0)),
    ]
    return pl.pallas_call(
        _ssd_prompt_kernel,
        grid=(bsz, SSM_GROUPS, nc),
        in_specs=in_specs,
        out_specs=[pl.BlockSpec((q, GROUP_WIDTH), lambda b, g, c: (b * nc + c, g)),
                   pl.BlockSpec((1, GROUP_WIDTH, D_STATE), lambda b, g, c: (b, g, 0))],
        out_shape=[jax.ShapeDtypeStruct((bsz * seq, d_inner), BF16),
                   jax.ShapeDtypeStruct((bsz, d_inner, D_STATE), F32)],
        scratch_shapes=[pltpu.VMEM((GROUP_WIDTH, D_STATE), F32),
                        pltpu.VMEM((SUBLANES + q, GROUP_WIDTH), F32),
                        pltpu.VMEM((SUBLANES + q, D_STATE), F32),
                        pltpu.VMEM((SUBLANES + q, D_STATE), F32)],
        compiler_params=_cparams("parallel", "parallel", "arbitrary"),
        name="ssd_prompt",
    )(zx, zx, zx, zx, dtg, acsg, acstg, conv_w, conv_w, conv_w, conv_b, conv_b, conv_b, dskx, gs)


def _ssd_sample_kernel(h_ref, xbc_ref, hist_ref, z_ref, dt_ref, acs_ref, acst_ref,
                       wc_ref, bc_ref, dsk_ref, gs_ref, *refs, n_new, aliased):
    if aliased:
        refs = refs[1:]
    o_ref, hn_ref, xp_ref, xa_ref, z16_ref = refs
    d_inner = SSM_GROUPS * GROUP_WIDTH
    xp_ref[...] = jnp.zeros_like(xp_ref)
    xp_ref[0:CONV_W - 1, :] = hist_ref[0]
    xp_ref[CONV_W - 1:CONV_W - 1 + n_new, :] = xbc_ref[0]
    xp = xp_ref[...]
    out = bc_ref[...]
    for j in range(CONV_W):
        out = out + xp[j:j + n_new] * wc_ref[j:j + 1, :]
    xa_ref[...] = jnp.zeros_like(xa_ref)
    xa_ref[0:n_new, :] = _silu(out)
    z16_ref[...] = jnp.zeros_like(z16_ref)
    z16_ref[0:n_new, :] = z_ref[0]
    dt16 = dt_ref[0]
    acs16 = acs_ref[0]
    acst = acst_ref[0]
    for g in range(SSM_GROUPS):
        cs = slice(g * GROUP_WIDTH, (g + 1) * GROUP_WIDTH)
        hs = slice(g * HEADS_PER_GROUP, (g + 1) * HEADS_PER_GROUP)
        bcol = d_inner + g * D_STATE
        ccol = d_inner + (SSM_GROUPS + g) * D_STATE
        y, hnew = _ssd_chunk(xa_ref[:, cs], xa_ref[:, bcol:bcol + D_STATE], xa_ref[:, ccol:ccol + D_STATE],
                             z16_ref[:, cs], dt16[:, hs], acs16[:, hs], acst[hs, :],
                             h_ref[0, cs, :], dsk_ref[:, cs], gs_ref[:, cs])
        for j, hj in enumerate(hnew):
            r0 = g * GROUP_WIDTH + j * SSM_HEAD_DIM
            hn_ref[0, 0, r0:r0 + SSM_HEAD_DIM, :] = hj
        o_ref[0, :, cs] = y


def ssd_sample(layer, depth, h0, xbc, hist, z, dt16, acs16, acst, conv_w, conv_b, dskx, gs, prev_states):
    _, db, hd, n = h0.shape
    n_new = xbc.shape[1]
    conv_dim = xbc.shape[2]
    d_inner = SSM_GROUPS * GROUP_WIDTH
    nh = d_inner // SSM_HEAD_DIM
    per_b = lambda *shape: pl.BlockSpec((1,) + shape, lambda b: (b,) + (0,) * len(shape))
    full = lambda *shape: pl.BlockSpec(shape, lambda b: (0,) * len(shape))
    aliased = prev_states is not None
    per_lb = lambda *shape: pl.BlockSpec((None, 1) + shape, lambda b: (layer, b) + (0,) * len(shape))
    per_l = lambda *shape: pl.BlockSpec((None,) + shape, lambda b: (layer,) + (0,) * len(shape))
    in_specs = [per_lb(hd, n), per_b(n_new, conv_dim), per_lb(CONV_W - 1, conv_dim), per_b(n_new, d_inner),
                per_b(TOK16, nh), per_b(TOK16, nh), per_b(nh, TOK16),
                per_l(CONV_W, conv_dim), per_l(1, conv_dim), full(1, d_inner), full(1, d_inner)]
    args = [h0, xbc, hist, z, dt16, acs16, acst, conv_w, conv_b, dskx, gs]
    if aliased:
        in_specs.append(pl.BlockSpec(memory_space=pl.ANY))
        args.append(prev_states)
    return pl.pallas_call(
        functools.partial(_ssd_sample_kernel, n_new=n_new, aliased=aliased),
        grid=(db,),
        in_specs=in_specs,
        out_specs=[per_b(TOK16, d_inner),
                   pl.BlockSpec((1, 1, hd, n), lambda b: (layer, b, 0, 0))],
        out_shape=[jax.ShapeDtypeStruct((db, TOK16, d_inner), F32),
                   jax.ShapeDtypeStruct((depth, db, hd, n), F32)],
        scratch_shapes=[pltpu.VMEM((SUBLANES, conv_dim), F32),
                        pltpu.VMEM((TOK16, conv_dim), F32),
                        pltpu.VMEM((TOK16, d_inner), F32)],
        input_output_aliases={len(args) - 1: 1} if aliased else {},
        compiler_params=_cparams("parallel"),
        name="ssd_sample",
    )(*args)


def _page_head(page_ref, hk):
    return page_ref[0, 0, pl.ds(hk, PAGE, stride=N_KV_HEADS), :]


def _attn_sample_kernel(pt_ref, q_ref, kn_ref, vn_ref, nb_ref, tm_ref, *refs, n_new, pps):
    del pt_ref
    k_refs, v_refs, lf_refs = refs[0:pps], refs[pps:2 * pps], refs[2 * pps:3 * pps]
    o_ref, qs_ref, knp_ref, m_ref, l_ref, acc_ref, carry_ref, r_ref = refs[3 * pps:]
    b = pl.program_id(0)
    p = pl.program_id(1)
    rows = KV_GROUP * n_new

    @pl.when((b == 0) & (p == 0))
    def _():
        knp_ref[...] = jnp.zeros_like(knp_ref)

    @pl.when(p == 0)
    def _():
        for head in range(N_HEADS):
            qs_ref[head * n_new:(head + 1) * n_new, :] = q_ref[0, :, head * HEAD_DIM:(head + 1) * HEAD_DIM]
        knp_ref[0:n_new, :] = kn_ref[0]
        m_ref[...] = jnp.full_like(m_ref, NEG_INF)
        l_ref[...] = jnp.zeros_like(l_ref)
        acc_ref[...] = jnp.zeros_like(acc_ref)
        carry_ref[...] = jnp.zeros_like(carry_ref)

    tm = tm_ref[...]
    qh = [qs_ref[hk * rows:(hk + 1) * rows, :].astype(BF16) for hk in range(N_KV_HEADS)]
    s_cols = []
    for u in range(pps):
        lft = lf_refs[u][0, 0]
        hi, mid, lo = _split3(lft)
        r16 = _dot(hi, tm) + _dot(mid, tm) + _dot(lo, tm) + carry_ref[...]
        carry_ref[...] = carry_ref[...] + jnp.sum(lft, axis=-1, keepdims=True)
        for head in range(N_HEADS):
            r_ref[head * n_new:(head + 1) * n_new, u * PAGE:(u + 1) * PAGE] = jnp.broadcast_to(
                r16[head:head + 1, :], (n_new, PAGE))
        s_cols.append(jnp.concatenate(
            [_dot_nt(qh[hk], _page_head(k_refs[u], hk).astype(BF16)) for hk in range(N_KV_HEADS)], axis=0))
    s = jnp.concatenate(s_cols, axis=1) + r_ref[...] * LOG2E
    m_old = m_ref[...]
    m_new = jnp.maximum(m_old, jnp.max(s, axis=-1, keepdims=True))
    pr = jnp.exp2(s - m_new)
    alpha = jnp.exp2(m_old - m_new)
    l_ref[...] = alpha * l_ref[...] + jnp.sum(pr, axis=-1, keepdims=True)
    m_ref[...] = m_new
    acc_old = acc_ref[...]
    for hk in range(N_KV_HEADS):
        rs = slice(hk * rows, (hk + 1) * rows)
        upd = alpha[rs, :] * acc_old[rs, :]
        for u in range(pps):
            upd = upd + _dot(pr[rs, u * PAGE:(u + 1) * PAGE].astype(BF16),
                             _page_head(v_refs[u], hk).astype(BF16))
        acc_ref[rs, :] = upd

    @pl.when(p == pl.num_programs(1) - 1)
    def _():
        sn = jnp.concatenate(
            [_dot_nt(qh[hk], knp_ref[:, hk * HEAD_DIM:(hk + 1) * HEAD_DIM].astype(BF16))
             for hk in range(N_KV_HEADS)], axis=0) + nb_ref[0] * LOG2E
        m_old = m_ref[...]
        m_new = jnp.maximum(m_old, jnp.max(sn, axis=-1, keepdims=True))
        pn = jnp.exp2(sn - m_new)
        alpha = jnp.exp2(m_old - m_new)
        l = alpha * l_ref[...] + jnp.sum(pn, axis=-1, keepdims=True)
        acc = alpha * acc_ref[...]
        vn = vn_ref[0]
        for hk in range(N_KV_HEADS):
            rs = slice(hk * rows, (hk + 1) * rows)
            upd = acc[rs, :]
            for j in range(n_new):
                upd = upd + pn[rs, j:j + 1] * vn[j:j + 1, hk * HEAD_DIM:(hk + 1) * HEAD_DIM]
            acc_ref[rs, :] = upd / l[rs, :]
        for head in range(N_HEADS):
            o_ref[0, :, head * HEAD_DIM:(head + 1) * HEAD_DIM] = acc_ref[head * n_new:(head + 1) * n_new, :]


def attn_sample(layer, page_table, q, kn, vn, nb, cache_k, cache_v, cache_lft):
    db, n_new, _ = q.shape
    n_pages = page_table.shape[1]
    pps = PAGES_PER_STEP
    kvw = N_KV_HEADS * HEAD_DIM
    n_rows = n_new * N_HEADS
    idx = np.arange(PAGE)
    tm = jnp.asarray(idx[:, None] > idx[None, :], BF16)
    pt_flat = page_table.reshape(-1)

    def page(u, ndim):
        def index_map(b, p, pt):
            return (layer, pt[b * n_pages + (n_pages - 1 - (p * pps + u))]) + (0,) * ndim
        return index_map

    per_b = lambda *shape: pl.BlockSpec((1,) + shape, lambda b, p, pt: (b,) + (0,) * len(shape))
    kv_spec = lambda u: pl.BlockSpec((1, 1, PAGE * N_KV_HEADS, HEAD_DIM), page(u, 2))
    lf_spec = lambda u: pl.BlockSpec((1, 1, N_HEADS, PAGE), page(u, 2))
    grid_spec = pltpu.PrefetchScalarGridSpec(
        num_scalar_prefetch=1,
        grid=(db, n_pages // pps),
        in_specs=[per_b(n_new, N_HEADS * HEAD_DIM), per_b(n_new, kvw), per_b(n_new, kvw),
                  per_b(n_rows, LANES),
                  pl.BlockSpec((PAGE, PAGE), lambda b, p, pt: (0, 0))]
                 + [kv_spec(u) for u in range(pps)] + [kv_spec(u) for u in range(pps)]
                 + [lf_spec(u) for u in range(pps)],
        out_specs=per_b(n_new, N_HEADS * HEAD_DIM),
        scratch_shapes=[pltpu.VMEM((n_rows, HEAD_DIM), F32),
                        pltpu.VMEM((PAGE, kvw), F32),
                        pltpu.VMEM((n_rows, 1), F32),
                        pltpu.VMEM((n_rows, 1), F32),
                        pltpu.VMEM((n_rows, HEAD_DIM), F32),
                        pltpu.VMEM((N_HEADS, 1), F32),
                        pltpu.VMEM((n_rows, pps * PAGE), F32)],
    )
    return pl.pallas_call(
        functools.partial(_attn_sample_kernel, n_new=n_new, pps=pps),
        grid_spec=grid_spec,
        out_shape=jax.ShapeDtypeStruct((db, n_new, N_HEADS * HEAD_DIM), F32),
        compiler_params=_cparams("arbitrary", "arbitrary"),
        name="attn_sample",
    )(pt_flat, q, kn, vn, nb, tm, *([cache_k] * pps), *([cache_v] * pps), *([cache_lft] * pps))


def kernel(x_prompt, x_sample, cache_k, cache_v, cache_logf, state_ssm, state_conv, page_table,
           p_prompt, p_sample, g_ffn1, w_ffn1_in, w_ffn1_out, g_mix, w_in, b_f, g_q, g_k,
           conv_w, conv_b, dt_bias, a_log, d_skip, g_ssm, w_branch_attn, w_branch_ssm, w_out,
           g_ffn2, w_ffn2_in, w_ffn2_out, g_ple, w_ple_gate, w_ple_proj):
    depth = w_in.shape[0]
    bp, seq, d_model = x_prompt.shape
    db, n_new, _ = x_sample.shape
    mp, ms = bp * seq, db * n_new
    attn_w = N_HEADS * HEAD_DIM
    kv_w = N_KV_HEADS * HEAD_DIM
    d_inner = SSM_GROUPS * GROUP_WIDTH
    conv_dim = d_inner + 2 * SSM_GROUPS * D_STATE
    ssm_heads = d_inner // SSM_HEAD_DIM
    assert KV_GROUP * n_new == SUBLANES and page_table.shape[1] % PAGES_PER_STEP == 0
    o_q, o_k, o_v = 0, attn_w, attn_w + kv_w
    o_f = o_v + kv_w
    o_z = o_f + N_HEADS
    o_x = o_z + d_inner
    o_dt = o_x + conv_dim
    o_ga = o_dt + ssm_heads
    z_col, xbc_col = 0, d_inner
    ga_col, gb_col = 0, d_model
    scale = HEAD_DIM ** -0.5

    n_pool = cache_k.shape[1]
    ck = cache_k.reshape(depth, n_pool, PAGE * N_KV_HEADS, HEAD_DIM)
    cv = cache_v.reshape(depth, n_pool, PAGE * N_KV_HEADS, HEAD_DIM)
    clft = jnp.swapaxes(cache_logf, 2, 3)

    state4 = state_ssm.reshape(depth, db, d_inner, D_STATE)
    conv_b3 = conv_b.reshape(depth, 1, conv_dim)

    h = jnp.concatenate([x_prompt.reshape(mp, d_model), x_sample.reshape(ms, d_model)], axis=0)
    outs = [[] for _ in range(9)]
    states_s = None
    wt = jnp.swapaxes(w_in, 1, 2)
    for i in range(depth):
        gate_bias = jnp.concatenate(
            [dt_bias[i], b_f[i], jnp.zeros((LANES - ssm_heads - N_HEADS,), F32)]).reshape(1, LANES)
        alog_row = jnp.concatenate([a_log[i], jnp.zeros((LANES - ssm_heads,), F32)]).reshape(1, LANES)
        dskx = jnp.repeat(d_skip[i], SSM_HEAD_DIM).reshape(1, d_inner)
        gs = g_ssm[i].reshape(1, d_inner)

        act = mm_swiglu(rmsnorm(h, g_ffn1[i]), w_ffn1_in, i)
        h = mm_residual(act, w_ffn1_out, i, h, 0.5, TM)

        u = rmsnorm(h, g_mix[i])
        (q_b,) = mm_headnorm(u, wt, i, g_q[i], [BF16], o_q, attn_w, mult=scale * LOG2E)
        k_f, k_b = mm_headnorm(u, wt, i, g_k[i], [F32, BF16], o_k, kv_w)
        v_f, v_b = mm_plain(u, wt, i, [F32, BF16], TN, o_v, kv_w)
        (wide,) = mm_plain(u, wt, i, [F32], 2 * TN, o_z, d_inner + conv_dim)
        (gate_pre,) = mm_plain(u, wt, i, [F32], 2 * TN, o_ga, 2 * d_model)
        small = mm_gate_logits(u, wt, i, o_dt, ssm_heads, o_f, N_HEADS)

        dt_p, lf_p, c_p, acs_p = gates(small[:mp], gate_bias, alog_row, seq, SSM_CHUNK)
        dt_s, lf_s, c_s, acs_s = gates(small[mp:], gate_bias, alog_row, n_new, n_new)
        fcols = slice(ssm_heads, ssm_heads + N_HEADS)

        ct = c_p[:, fcols].reshape(bp, seq, N_KV_HEADS, KV_GROUP).transpose(0, 2, 3, 1)
        ct = jnp.pad(ct, ((0, 0), (0, 0), (0, SUBLANES - KV_GROUP), (0, 0)))
        o_attn_p = attn_prompt(q_b, k_b, v_b, ct, bp, seq)
        pg = lambda a: a[:, :ssm_heads].reshape(bp, seq, SSM_GROUPS, HEADS_PER_GROUP).transpose(0, 2, 1, 3)
        acsg = pg(acs_p)
        o_ssm_p, st_p = ssd_prompt(i, wide, z_col, xbc_col, pg(dt_p), acsg, acsg.transpose(0, 1, 3, 2),
                                   conv_w, conv_b3, dskx, gs, bp, seq)

        wide_s = wide[mp:].reshape(db, n_new, -1)
        xbc_s = wide_s[:, :, xbc_col:xbc_col + conv_dim]
        tail = TOK16 - n_new
        dt16 = jnp.pad(dt_s[:, :ssm_heads].reshape(db, n_new, ssm_heads), ((0, 0), (0, tail), (0, 0)))
        acs16 = jnp.pad(acs_s[:, :ssm_heads].reshape(db, n_new, ssm_heads), ((0, 0), (0, tail), (0, 0)),
                        mode="edge")
        o_ssm_s16, states_s = ssd_sample(
            i, depth, state4, xbc_s, state_conv, wide_s[:, :, z_col:z_col + d_inner],
            dt16, acs16, acs16.transpose(0, 2, 1), conv_w, conv_b3, dskx, gs, states_s)
        o_ssm_s = o_ssm_s16[:, :n_new].reshape(ms, d_inner).astype(BF16)

        cn = c_s[:, fcols].reshape(db, n_new, N_HEADS)
        tt = np.arange(n_new)
        nbias = jnp.where((tt[None, :] <= tt[:, None])[None, None, :, :],
                          -cn.transpose(0, 2, 1)[:, :, None, :], NEG_INF)
        nbias = jnp.pad(nbias.reshape(db, N_HEADS * n_new, n_new),
                        ((0, 0), (0, 0), (0, LANES - n_new)), constant_values=NEG_INF)
        o_attn_s = attn_sample(i, page_table,
                               q_b[mp:].astype(F32).reshape(db, n_new, attn_w),
                               k_b[mp:].astype(F32).reshape(db, n_new, kv_w),
                               v_f[mp:].reshape(db, n_new, kv_w),
                               nbias, ck, cv, clft)
        o_attn_s = o_attn_s.reshape(ms, attn_w).astype(BF16)

        merged = mm_merge(o_attn_p, o_attn_s, w_branch_attn, o_ssm_p, o_ssm_s, w_branch_ssm, i,
                          gate_pre, ga_col, gb_col)
        h = mm_residual(merged, w_out, i, h, 1.0, TM_WIDE)

        act = mm_swiglu(rmsnorm(h, g_ffn2[i]), w_ffn2_in, i)
        h = mm_residual(act, w_ffn2_out, i, h, 0.5, TM)

        p_l = jnp.concatenate([p_prompt[i].reshape(mp, -1), p_sample[i].reshape(ms, -1)], axis=0).astype(BF16)
        h = mm_ple(rmsnorm(h, g_ple[i]), w_ple_gate, p_l, w_ple_proj, i, h)

        conv_p = jnp.stack([wide[(b + 1) * seq - (CONV_W - 1):(b + 1) * seq, xbc_col:xbc_col + conv_dim]
                            for b in range(bp)])
        layer_out = (
            k_f[:mp].reshape(bp, seq, N_KV_HEADS, HEAD_DIM), v_f[:mp].reshape(bp, seq, N_KV_HEADS, HEAD_DIM),
            lf_p[:, fcols].reshape(bp, seq, N_HEADS), conv_p,
            st_p.reshape(bp, ssm_heads, SSM_HEAD_DIM, D_STATE),
            k_f[mp:].reshape(db, n_new, N_KV_HEADS, HEAD_DIM), v_f[mp:].reshape(db, n_new, N_KV_HEADS, HEAD_DIM),
            lf_s[:, fcols].reshape(db, n_new, N_HEADS),
            jnp.concatenate([state_conv[i], xbc_s], axis=1)[:, n_new:n_new + CONV_W - 1])
        for lst, val in zip(outs, layer_out):
            lst.append(val)

    stacked = [jnp.stack(lst) for lst in outs]
    return (h[:mp].reshape(bp, seq, d_model), h[mp:].reshape(db, n_new, d_model), *stacked,
            states_s.reshape(depth, db, ssm_heads, SSM_HEAD_DIM, D_STATE))
```

```python
import functools

import numpy as np
import jax
import jax.numpy as jnp
from jax import lax
from jax.experimental import pallas as pl
from jax.experimental.pallas import tpu as pltpu

F32 = jnp.float32
BF16 = jnp.bfloat16
EPS = 1e-6
NEG_INF = float("-inf")

LANES = 128
SUBLANES = 8
VMEM_LIMIT_BYTES = 56 * 1024 * 1024

HEAD_DIM = 128
N_HEADS = 16
N_KV_HEADS = 8
KV_GROUP = N_HEADS // N_KV_HEADS
SSM_HEAD_DIM = 64
SSM_GROUPS = 8
HEADS_PER_GROUP = 8
GROUP_WIDTH = HEADS_PER_GROUP * SSM_HEAD_DIM
D_STATE = 128
CONV_W = 4
SSM_CHUNK = 128
GROUPS_PER_STEP = 8
PAGE = 128

TM = 512
TM_WIDE = 1088
TN = 512
TQ = 1024
LOG2E = 1.4426950408889634
TOK16 = 16
PAGES_PER_STEP = 8


def _cparams(*sem):
    return pltpu.CompilerParams(dimension_semantics=sem, vmem_limit_bytes=VMEM_LIMIT_BYTES)


def _sigmoid(x):
    return 0.5 + 0.5 * jnp.tanh(0.5 * x)


def _silu(x):
    return x * _sigmoid(x)


def _dot(a, b):
    return jnp.dot(a, b, preferred_element_type=F32)


def _dot_nt(a, b):
    return lax.dot_general(a, b, (((1,), (1,)), ((), ())), preferred_element_type=F32)


def _dot_tn(a, b):
    return lax.dot_general(a, b, (((0,), (0,)), ((), ())), preferred_element_type=F32)


def _split3(x):
    hi = x.astype(BF16)
    r1 = x - hi.astype(F32)
    mid = r1.astype(BF16)
    lo = (r1 - mid.astype(F32)).astype(BF16)
    return hi, mid, lo


def _rmsnorm_kernel(x_ref, g_ref, o_ref):
    x = x_ref[...]
    ms = jnp.mean(x * x, axis=-1, keepdims=True)
    o_ref[...] = (x * lax.rsqrt(ms + EPS) * g_ref[...]).astype(o_ref.dtype)


def rmsnorm(x, g):
    m, d = x.shape
    return pl.pallas_call(
        _rmsnorm_kernel,
        grid=(m // TM,),
        in_specs=[pl.BlockSpec((TM, d), lambda i: (i, 0)),
                  pl.BlockSpec((1, d), lambda i: (0, 0))],
        out_specs=pl.BlockSpec((TM, d), lambda i: (i, 0)),
        out_shape=jax.ShapeDtypeStruct((m, d), BF16),
        compiler_params=_cparams("parallel"),
        name="rmsnorm",
    )(x, g.reshape(1, d))


def _cast_weights(pairs):
    @pl.when(pl.program_id(1) == 0)
    def _():
        for w_ref, ws_ref in pairs:
            ws_ref[...] = w_ref[...].reshape(ws_ref.shape).astype(BF16)


_MM_SEM = ("parallel", "arbitrary")


def _wspec(w, layer, k, tn, j0=0):
    if w.ndim == 2:
        return pl.BlockSpec((k, tn), lambda j, i: (0, j + j0))
    return pl.BlockSpec((None, k, tn), lambda j, i: (layer, 0, j + j0))


def _swiglu_kernel(a_ref, wa_ref, wb_ref, o_ref, wsa_ref, wsb_ref):
    _cast_weights([(wa_ref, wsa_ref), (wb_ref, wsb_ref)])
    a = a_ref[...]
    o_ref[...] = (_silu(_dot(a, wsa_ref[...])) * _dot(a, wsb_ref[...])).astype(o_ref.dtype)


def mm_swiglu(a, w, layer):
    m, k = a.shape
    dff = w.shape[-1] // 2
    nj = dff // TN
    tm = TM_WIDE
    return pl.pallas_call(
        _swiglu_kernel,
        grid=(nj, m // tm),
        in_specs=[pl.BlockSpec((tm, k), lambda j, i: (i, 0)),
                  _wspec(w, layer, k, TN), _wspec(w, layer, k, TN, nj)],
        out_specs=pl.BlockSpec((tm, TN), lambda j, i: (i, j)),
        out_shape=jax.ShapeDtypeStruct((m, dff), BF16),
        scratch_shapes=[pltpu.VMEM((k, TN), BF16), pltpu.VMEM((k, TN), BF16)],
        compiler_params=_cparams(*_MM_SEM),
        name="mm_swiglu",
    )(a, w, w)


def _residual_kernel(a_ref, w_ref, r_ref, o_ref, ws_ref, *, alpha):
    _cast_weights([(w_ref, ws_ref)])
    o_ref[...] = r_ref[...] + alpha * _dot(a_ref[...], ws_ref[...])


def mm_residual(a, w, layer, res, alpha, tm):
    m, k = a.shape
    n = w.shape[-1]
    return pl.pallas_call(
        functools.partial(_residual_kernel, alpha=alpha),
        grid=(n // TN, m // tm),
        in_specs=[pl.BlockSpec((tm, k), lambda j, i: (i, 0)),
                  _wspec(w, layer, k, TN),
                  pl.BlockSpec((tm, TN), lambda j, i: (i, j))],
        out_specs=pl.BlockSpec((tm, TN), lambda j, i: (i, j)),
        out_shape=jax.ShapeDtypeStruct((m, n), F32),
        scratch_shapes=[pltpu.VMEM((k, TN), BF16)],
        compiler_params=_cparams(*_MM_SEM),
        name="mm_residual",
    )(a, w, res)


def _wtspec(layer, k, rows, row0, stride=0):
    return pl.BlockSpec((pl.Element(1), pl.Element(rows), pl.Element(k)),
                        lambda j, i: (layer, pl.multiple_of(row0 + j * stride, SUBLANES), 0))


def _plain_kernel(a_ref, w_ref, *refs):
    *o_refs, ws_ref = refs
    _cast_weights([(w_ref, ws_ref)])
    acc = _dot_nt(a_ref[...], ws_ref[...])
    for o_ref in o_refs:
        o_ref[...] = acc.astype(o_ref.dtype)


def mm_plain(a, wt, layer, out_dtypes, tn, row0, n):
    m, k = a.shape
    tm = TM_WIDE
    return pl.pallas_call(
        _plain_kernel,
        grid=(n // tn, m // tm),
        in_specs=[pl.BlockSpec((tm, k), lambda j, i: (i, 0)),
                  _wtspec(layer, k, tn, row0, tn)],
        out_specs=[pl.BlockSpec((tm, tn), lambda j, i: (i, j)) for _ in out_dtypes],
        out_shape=[jax.ShapeDtypeStruct((m, n), dt) for dt in out_dtypes],
        scratch_shapes=[pltpu.VMEM((tn, k), BF16)],
        compiler_params=_cparams(*_MM_SEM),
        name="mm_plain",
    )(a, wt)


def _gate_logits_kernel(a_ref, wdt_ref, wf_ref, o_ref, ws_ref):
    @pl.when(pl.program_id(1) == 0)
    def _():
        ndt, nf = wdt_ref.shape[1], wf_ref.shape[1]
        ws_ref[...] = jnp.zeros_like(ws_ref)
        ws_ref[0:ndt, :] = wdt_ref[0].astype(BF16)
        ws_ref[ndt:ndt + nf, :] = wf_ref[0].astype(BF16)

    o_ref[...] = _dot_nt(a_ref[...], ws_ref[...])


def mm_gate_logits(a, wt, layer, dt_row0, n_dt, f_row0, n_f):
    m, k = a.shape
    tm = TM_WIDE
    return pl.pallas_call(
        _gate_logits_kernel,
        grid=(1, m // tm),
        in_specs=[pl.BlockSpec((tm, k), lambda j, i: (i, 0)),
                  _wtspec(layer, k, n_dt, dt_row0), _wtspec(layer, k, n_f, f_row0)],
        out_specs=pl.BlockSpec((tm, LANES), lambda j, i: (i, 0)),
        out_shape=jax.ShapeDtypeStruct((m, LANES), F32),
        scratch_shapes=[pltpu.VMEM((LANES, k), BF16)],
        compiler_params=_cparams(*_MM_SEM),
        name="mm_gate_logits",
    )(a, wt, wt)


def _headnorm_kernel(a_ref, w_ref, g_ref, *refs, mult):
    *o_refs, ws_ref = refs
    _cast_weights([(w_ref, ws_ref)])
    acc = _dot_nt(a_ref[...], ws_ref[...])
    g = g_ref[...]
    for hh in range(acc.shape[1] // HEAD_DIM):
        sl = slice(hh * HEAD_DIM, (hh + 1) * HEAD_DIM)
        blk = acc[:, sl]
        ms = jnp.mean(blk * blk, axis=-1, keepdims=True)
        y = blk * lax.rsqrt(ms + EPS) * g
        for o_ref in o_refs:
            o_ref[:, sl] = (y * mult).astype(o_ref.dtype) if o_ref.dtype == BF16 else y


def mm_headnorm(a, wt, layer, g, out_dtypes, row0, n, mult=1.0):
    m, k = a.shape
    tm = TM_WIDE
    return pl.pallas_call(
        functools.partial(_headnorm_kernel, mult=mult),
        grid=(n // TN, m // tm),
        in_specs=[pl.BlockSpec((tm, k), lambda j, i: (i, 0)),
                  _wtspec(layer, k, TN, row0, TN),
                  pl.BlockSpec((1, HEAD_DIM), lambda j, i: (0, 0))],
        out_specs=[pl.BlockSpec((tm, TN), lambda j, i: (i, j)) for _ in out_dtypes],
        out_shape=[jax.ShapeDtypeStruct((m, n), dt) for dt in out_dtypes],
        scratch_shapes=[pltpu.VMEM((TN, k), BF16)],
        compiler_params=_cparams(*_MM_SEM),
        name="mm_headnorm",
    )(a, wt, g.reshape(1, HEAD_DIM))


def _merge_kernel(a1p_ref, a1s_ref, w1_ref, a2p_ref, a2s_ref, w2_ref, ga_ref, gb_ref, o_ref, ws1_ref, ws2_ref,
                  *, prompt_tiles):
    _cast_weights([(w1_ref, ws1_ref), (w2_ref, ws2_ref)])

    def emit(a1_ref, a2_ref):
        o_ref[...] = (_sigmoid(ga_ref[...]) * _dot(a1_ref[...], ws1_ref[...])
                      + _sigmoid(gb_ref[...]) * _dot(a2_ref[...], ws2_ref[...])).astype(o_ref.dtype)

    @pl.when(pl.program_id(1) < prompt_tiles)
    def _():
        emit(a1p_ref, a2p_ref)

    @pl.when(pl.program_id(1) >= prompt_tiles)
    def _():
        emit(a1s_ref, a2s_ref)


def mm_merge(a1p, a1s, w1, a2p, a2s, w2, layer, gates, ga_col, gb_col):
    mp, k1 = a1p.shape
    k2 = a2p.shape[1]
    n = w1.shape[-1]
    assert a1s.shape[0] == TM and a2s.shape[0] == TM
    npt = mp // TM
    ja, jb = ga_col // TN, gb_col // TN
    prompt = lambda k: pl.BlockSpec((TM, k), lambda j, i: (jnp.minimum(i, npt - 1), 0))
    sample = lambda k: pl.BlockSpec((TM, k), lambda j, i: (0, 0))
    return pl.pallas_call(
        functools.partial(_merge_kernel, prompt_tiles=npt),
        grid=(n // TN, npt + 1),
        in_specs=[prompt(k1), sample(k1), _wspec(w1, layer, k1, TN),
                  prompt(k2), sample(k2), _wspec(w2, layer, k2, TN),
                  pl.BlockSpec((TM, TN), lambda j, i: (i, j + ja)),
                  pl.BlockSpec((TM, TN), lambda j, i: (i, j + jb))],
        out_specs=pl.BlockSpec((TM, TN), lambda j, i: (i, j)),
        out_shape=jax.ShapeDtypeStruct((mp + TM, n), BF16),
        scratch_shapes=[pltpu.VMEM((k1, TN), BF16), pltpu.VMEM((k2, TN), BF16)],
        compiler_params=_cparams(*_MM_SEM),
        name="mm_merge",
    )(a1p, a1s, w1, a2p, a2s, w2, gates, gates)


def _ple_kernel(a_ref, wg_ref, p_ref, wp_ref, r_ref, o_ref, wsg_ref, wsp_ref):
    _cast_weights([(wg_ref, wsg_ref), (wp_ref, wsp_ref)])
    gate = _sigmoid(_dot(a_ref[...], wsg_ref[...]))
    o_ref[...] = r_ref[...] + gate * _dot(p_ref[...], wsp_ref[...])


def mm_ple(a, wg, p, wp, layer, res):
    m, k = a.shape
    kp = p.shape[1]
    n = wg.shape[-1]
    tm = TM_WIDE
    return pl.pallas_call(
        _ple_kernel,
        grid=(n // TN, m // tm),
        in_specs=[pl.BlockSpec((tm, k), lambda j, i: (i, 0)),
                  _wspec(wg, layer, k, TN),
                  pl.BlockSpec((tm, kp), lambda j, i: (i, 0)),
                  _wspec(wp, layer, kp, TN),
                  pl.BlockSpec((tm, TN), lambda j, i: (i, j))],
        out_specs=pl.BlockSpec((tm, TN), lambda j, i: (i, j)),
        out_shape=jax.ShapeDtypeStruct((m, n), F32),
        scratch_shapes=[pltpu.VMEM((k, TN), BF16), pltpu.VMEM((kp, TN), BF16)],
        compiler_params=_cparams(*_MM_SEM),
        name="mm_ple",
    )(a, wg, p, wp, res)


def _tri_cumsum(t, x):
    hi, mid, lo = _split3(x)
    return _dot(t, hi) + _dot(t, mid) + _dot(t, lo)


def _gates_kernel(x_ref, bias_ref, alog_ref, tseq_ref, tchk_ref,
                  dt_ref, lf_ref, c_ref, acs_ref, carry_ref, *, blocks_per_seq):
    i = pl.program_id(0)
    xb = x_ref[...] + bias_ref[...]
    t = jnp.log1p(jnp.exp(-jnp.abs(xb)))
    sp = jnp.maximum(xb, 0.0) + t
    ls = jnp.minimum(xb, 0.0) - t
    dt_ref[...] = sp
    lf_ref[...] = ls
    a = sp * (-jnp.exp(alog_ref[...]))
    acs_ref[...] = _tri_cumsum(tchk_ref[...], a)

    @pl.when(i % blocks_per_seq == 0)
    def _():
        carry_ref[...] = jnp.zeros_like(carry_ref)

    c = _tri_cumsum(tseq_ref[...], ls) + carry_ref[...]
    c_ref[...] = c
    carry_ref[...] = c[TM - 1:TM, :]


def gates(x, bias, alog, seq_len, chunk):
    m = x.shape[0]
    idx = np.arange(TM)
    low = idx[:, None] >= idx[None, :]
    tseq = jnp.asarray(low & (idx[:, None] // seq_len == idx[None, :] // seq_len), BF16)
    tchk = jnp.asarray(low & (idx[:, None] // chunk == idx[None, :] // chunk), BF16)
    row = pl.BlockSpec((TM, LANES), lambda i: (i, 0))
    vec = pl.BlockSpec((1, LANES), lambda i: (0, 0))
    tri = pl.BlockSpec((TM, TM), lambda i: (0, 0))
    return pl.pallas_call(
        functools.partial(_gates_kernel, blocks_per_seq=max(1, seq_len // TM)),
        grid=(m // TM,),
        in_specs=[row, vec, vec, tri, tri],
        out_specs=[row, row, row, row],
        out_shape=[jax.ShapeDtypeStruct((m, LANES), F32)] * 4,
        scratch_shapes=[pltpu.VMEM((1, LANES), F32)],
        compiler_params=_cparams("arbitrary"),
        name="gates",
    )(x, bias, alog, tseq, tchk)


def _attn_prompt_kernel(q_ref, k_ref, v_ref, ct_ref, o_ref):
    qi = pl.program_id(2)
    q0 = pl.multiple_of(qi * TQ, TQ)
    crefs = [ct_ref[0, 0, g:g + 1, pl.ds(q0, LANES)][:, 0:1] for g in range(KV_GROUP)]
    half = TQ // 2

    def tile(g, r0, nr, ks, nk, m, l, acc, diag_off=None):
        q = q_ref[r0:r0 + nr, g * HEAD_DIM:(g + 1) * HEAD_DIM]
        kb = k_ref[pl.ds(ks, nk), :]
        vb = v_ref[pl.ds(ks, nk), :]
        s = _dot_nt(q, kb) + (crefs[g] - ct_ref[0, 0, g:g + 1, pl.ds(ks, nk)]) * LOG2E
        if diag_off is not None:
            row = lax.broadcasted_iota(jnp.int32, (nr, nk), 0) + r0
            col = lax.broadcasted_iota(jnp.int32, (nr, nk), 1) + diag_off
            s = jnp.where(col <= row, s, NEG_INF)
        m_new = jnp.maximum(m, jnp.max(s, axis=-1, keepdims=True))
        p = jnp.exp2(s - m_new)
        alpha = jnp.exp2(m - m_new)
        l = alpha * l + jnp.sum(p, axis=-1, keepdims=True)
        acc = alpha * acc + _dot(p.astype(BF16), vb)
        return m_new, l, acc

    def body(ki, carry):
        ks = pl.multiple_of(ki * TQ, TQ)
        return tuple(tile(g, 0, TQ, ks, TQ, *carry[g]) for g in range(KV_GROUP))

    init = tuple((jnp.full((TQ, 1), NEG_INF, F32), jnp.zeros((TQ, 1), F32), jnp.zeros((TQ, HEAD_DIM), F32))
                 for _ in range(KV_GROUP))
    carry = lax.fori_loop(0, qi, body, init)
    first = [tile(g, 0, TQ, q0, half, *carry[g], diag_off=0) for g in range(KV_GROUP)]
    for g in range(KV_GROUP):
        m, l, acc = first[g]
        hs = slice(g * HEAD_DIM, (g + 1) * HEAD_DIM)
        o_ref[0:half, hs] = (acc[0:half] / l[0:half]).astype(o_ref.dtype)
        m, l, acc = tile(g, half, half, pl.multiple_of(q0 + half, half), half,
                         m[half:], l[half:], acc[half:], diag_off=half)
        o_ref[half:TQ, hs] = (acc / l).astype(o_ref.dtype)


def attn_prompt(q, k, v, ct, bsz, seq):
    nq = seq // TQ
    qw = KV_GROUP * HEAD_DIM
    return pl.pallas_call(
        _attn_prompt_kernel,
        grid=(bsz, N_KV_HEADS, nq),
        in_specs=[pl.BlockSpec((TQ, qw), lambda b, h, i: (b * nq + i, h)),
                  pl.BlockSpec((seq, HEAD_DIM), lambda b, h, i: (b, h)),
                  pl.BlockSpec((seq, HEAD_DIM), lambda b, h, i: (b, h)),
                  pl.BlockSpec((1, 1, SUBLANES, seq), lambda b, h, i: (b, h, 0, 0))],
        out_specs=pl.BlockSpec((TQ, qw), lambda b, h, i: (b * nq + i, h)),
        out_shape=jax.ShapeDtypeStruct((bsz * seq, N_HEADS * HEAD_DIM), BF16),
        compiler_params=_cparams("parallel", "parallel", "arbitrary"),
        name="attn_prompt",
    )(q, k, v, ct)


def _expand_heads(a):
    lane = lax.broadcasted_iota(jnp.int32, (a.shape[0], LANES), 1)
    blocks = [jnp.where(lane < SSM_HEAD_DIM, a[:, 2 * jj:2 * jj + 1], a[:, 2 * jj + 1:2 * jj + 2])
              for jj in range(HEADS_PER_GROUP // 2)]
    return jnp.concatenate(blocks, axis=1)


def _ssd_chunk(xs, bm, cm, z, dt8, acs8, acst, hprev, dsk, gs):
    q = xs.shape[0]
    acs8 = acs8 * LOG2E
    acst = acst * LOG2E
    dtx = _expand_heads(dt8)
    acsx = _expand_heads(acs8)
    acs_end = acsx[q - 1:q, :]
    xdt = xs * dtx
    cmb = cm.astype(BF16)
    bmb = bm.astype(BF16)
    cb = _dot_nt(cmb, bmb)
    row = lax.broadcasted_iota(jnp.int32, (q, q), 0)
    col = lax.broadcasted_iota(jnp.int32, (q, q), 1)
    causal = row >= col
    lane = lax.broadcasted_iota(jnp.int32, (q, LANES), 1)
    yblocks = []
    for jj in range(HEADS_PER_GROUP // 2):
        xblk = xdt[:, jj * LANES:(jj + 1) * LANES]
        acc = None
        for e in range(2):
            j = 2 * jj + e
            seg = acs8[:, j:j + 1] - acst[j:j + 1, :]
            lmat = jnp.exp2(jnp.where(causal, seg, NEG_INF))
            mh = (cb * lmat).astype(BF16)
            keep = (lane < SSM_HEAD_DIM) if e == 0 else (lane >= SSM_HEAD_DIM)
            part = _dot(mh, jnp.where(keep, xblk, 0.0).astype(BF16))
            acc = part if acc is None else acc + part
        yblocks.append(acc)
    y_diag = jnp.concatenate(yblocks, axis=1)
    y_off = _dot_nt(cmb, hprev.astype(BF16)) * jnp.exp2(acsx)
    xw = (xdt * jnp.exp2(acs_end - acsx)).astype(BF16)
    states = _dot_tn(xw, bmb)
    hnew = []
    for j in range(HEADS_PER_GROUP):
        sl = slice(j * SSM_HEAD_DIM, (j + 1) * SSM_HEAD_DIM)
        hnew.append(hprev[sl, :] * jnp.exp2(acst[j:j + 1, q - 1:q]) + states[sl, :])
    y = y_diag + y_off + dsk * xs
    y = y * _silu(z)
    ms = jnp.mean(y * y, axis=-1, keepdims=True)
    return y * lax.rsqrt(ms + EPS) * gs, hnew


def _conv_silu(cur_ref, buf_ref, w_ref, b_ref):
    rows = cur_ref.shape[0]
    cur = cur_ref[...]
    full = jnp.concatenate([buf_ref[0:SUBLANES, :], cur], axis=0)
    out = b_ref[...]
    for j in range(CONV_W - 1):
        out = out + pltpu.roll(full, CONV_W - 1 - j, 0)[SUBLANES:SUBLANES + rows] * w_ref[j:j + 1, :]
    out = out + cur * w_ref[CONV_W - 1:CONV_W, :]
    buf_ref[0:SUBLANES, :] = cur[rows - SUBLANES:rows]
    return _silu(out)


def _ssd_prompt_kernel(x_ref, b_ref, c_ref, z_ref, dt_ref, acs_ref, acst_ref,
                       wx_ref, wb_ref, wc_ref, bx_ref, bb_ref, bc_ref, dsk_ref, gs_ref,
                       o_ref, st_ref, h_ref, px_ref, pb_ref, pc_ref):
    ci = pl.program_id(2)

    @pl.when(ci == 0)
    def _():
        h_ref[...] = jnp.zeros_like(h_ref)
        for buf_ref in (px_ref, pb_ref, pc_ref):
            buf_ref[0:SUBLANES, :] = jnp.zeros((SUBLANES, buf_ref.shape[1]), F32)

    xs = _conv_silu(x_ref, px_ref, wx_ref, bx_ref)
    bm = _conv_silu(b_ref, pb_ref, wb_ref, bb_ref)
    cm = _conv_silu(c_ref, pc_ref, wc_ref, bc_ref)
    for gg in range(GROUPS_PER_STEP):
        cs = slice(gg * GROUP_WIDTH, (gg + 1) * GROUP_WIDTH)
        ns = slice(gg * D_STATE, (gg + 1) * D_STATE)
        y, hnew = _ssd_chunk(xs[:, cs], bm[:, ns], cm[:, ns], z_ref[:, cs], dt_ref[0, gg], acs_ref[0, gg],
                             acst_ref[0, gg], h_ref[cs, :], dsk_ref[:, cs], gs_ref[:, cs])
        for j, hj in enumerate(hnew):
            r0 = gg * GROUP_WIDTH + j * SSM_HEAD_DIM
            h_ref[r0:r0 + SSM_HEAD_DIM, :] = hj
        o_ref[:, cs] = y.astype(o_ref.dtype)

    @pl.when(ci == pl.num_programs(2) - 1)
    def _():
        st_ref[0] = h_ref[...]


def ssd_prompt(layer, zx, z_col, xbc_col, dtg, acsg, acstg, conv_w, conv_b, dskx, gs, bsz, seq):
    nc = seq // SSM_CHUNK
    d_inner = SSM_GROUPS * GROUP_WIDTH
    gps = GROUPS_PER_STEP
    gw, nw = gps * GROUP_WIDTH, gps * D_STATE
    jz = z_col // gw
    jx = xbc_col // gw
    jb = (xbc_col + d_inner) // nw
    jc = jb + SSM_GROUPS // gps
    wjb = d_inner // nw
    wjc = wjb + SSM_GROUPS // gps
    q = SSM_CHUNK
    rowmap = lambda off: (lambda b, g, c: (b * nc + c, off + g))
    wmap = lambda off: (lambda b, g, c: (0, off + g))
    lmap = lambda off: (lambda b, g, c: (layer, 0, off + g))
    small = pl.BlockSpec((1, gps, q, HEADS_PER_GROUP), lambda b, g, c: (b, g, c, 0))
    in_specs = [
        pl.BlockSpec((q, gw), rowmap(jx)),
        pl.BlockSpec((q, nw), rowmap(jb)),
        pl.BlockSpec((q, nw), rowmap(jc)),
        pl.BlockSpec((q, gw), rowmap(jz)),
        small, small,
        pl.BlockSpec((1, gps, HEADS_PER_GROUP, q), lambda b, g, c: (b, g, 0, c)),
        pl.BlockSpec((None, CONV_W, gw), lmap(0)),
        pl.BlockSpec((None, CONV_W, nw), lmap(wjb)),
        pl.BlockSpec((None, CONV_W, nw), lmap(wjc)),
        pl.BlockSpec((None, 1, gw), lmap(0)),
        pl.BlockSpec((None, 1, nw), lmap(wjb)),
        pl.BlockSpec((None, 1, nw), lmap(wjc)),
        pl.BlockSpec((1, gw), wmap(0)),
        pl.BlockSpec((1, gw), wmap(0)),
    ]
    return pl.pallas_call(
        _ssd_prompt_kernel,
        grid=(bsz, SSM_GROUPS // gps, nc),
        in_specs=in_specs,
        out_specs=[pl.BlockSpec((q, gw), lambda b, g, c: (b * nc + c, g)),
                   pl.BlockSpec((1, gw, D_STATE), lambda b, g, c: (b, g, 0))],
        out_shape=[jax.ShapeDtypeStruct((bsz * seq, d_inner), BF16),
                   jax.ShapeDtypeStruct((bsz, d_inner, D_STATE), F32)],
        scratch_shapes=[pltpu.VMEM((gw, D_STATE), F32),
                        pltpu.VMEM((SUBLANES + q, gw), F32),
                        pltpu.VMEM((SUBLANES + q, nw), F32),
                        pltpu.VMEM((SUBLANES + q, nw), F32)],
        compiler_params=_cparams("parallel", "parallel", "arbitrary"),
        name="ssd_prompt",
    )(zx, zx, zx, zx, dtg, acsg, acstg, conv_w, conv_w, conv_w, conv_b, conv_b, conv_b, dskx, gs)


def _ssd_sample_kernel(h_ref, xbc_ref, hist_ref, z_ref, dt_ref, acs_ref, acst_ref,
                       wc_ref, bc_ref, dsk_ref, gs_ref, *refs, n_new, aliased):
    if aliased:
        refs = refs[1:]
    o_ref, hn_ref, xp_ref, xa_ref, z16_ref = refs
    d_inner = SSM_GROUPS * GROUP_WIDTH
    xp_ref[...] = jnp.zeros_like(xp_ref)
    xp_ref[0:CONV_W - 1, :] = hist_ref[0]
    xp_ref[CONV_W - 1:CONV_W - 1 + n_new, :] = xbc_ref[0]
    xp = xp_ref[...]
    out = bc_ref[...]
    for j in range(CONV_W):
        out = out + xp[j:j + n_new] * wc_ref[j:j + 1, :]
    xa_ref[...] = jnp.zeros_like(xa_ref)
    xa_ref[0:n_new, :] = _silu(out)
    z16_ref[...] = jnp.zeros_like(z16_ref)
    z16_ref[0:n_new, :] = z_ref[0]
    dt16 = dt_ref[0]
    acs16 = acs_ref[0]
    acst = acst_ref[0]
    for g in range(SSM_GROUPS):
        cs = slice(g * GROUP_WIDTH, (g + 1) * GROUP_WIDTH)
        hs = slice(g * HEADS_PER_GROUP, (g + 1) * HEADS_PER_GROUP)
        bcol = d_inner + g * D_STATE
        ccol = d_inner + (SSM_GROUPS + g) * D_STATE
        y, hnew = _ssd_chunk(xa_ref[:, cs], xa_ref[:, bcol:bcol + D_STATE], xa_ref[:, ccol:ccol + D_STATE],
                             z16_ref[:, cs], dt16[:, hs], acs16[:, hs], acst[hs, :],
                             h_ref[0, cs, :], dsk_ref[:, cs], gs_ref[:, cs])
        for j, hj in enumerate(hnew):
            r0 = g * GROUP_WIDTH + j * SSM_HEAD_DIM
            hn_ref[0, 0, r0:r0 + SSM_HEAD_DIM, :] = hj
        o_ref[0, :, cs] = y


def ssd_sample(layer, depth, h0, xbc, hist, z, dt16, acs16, acst, conv_w, conv_b, dskx, gs, prev_states):
    _, db, hd, n = h0.shape
    n_new = xbc.shape[1]
    conv_dim = xbc.shape[2]
    d_inner = SSM_GROUPS * GROUP_WIDTH
    nh = d_inner // SSM_HEAD_DIM
    per_b = lambda *shape: pl.BlockSpec((1,) + shape, lambda b: (b,) + (0,) * len(shape))
    full = lambda *shape: pl.BlockSpec(shape, lambda b: (0,) * len(shape))
    aliased = prev_states is not None
    per_lb = lambda *shape: pl.BlockSpec((None, 1) + shape, lambda b: (layer, b) + (0,) * len(shape))
    per_l = lambda *shape: pl.BlockSpec((None,) + shape, lambda b: (layer,) + (0,) * len(shape))
    in_specs = [per_lb(hd, n), per_b(n_new, conv_dim), per_lb(CONV_W - 1, conv_dim), per_b(n_new, d_inner),
                per_b(TOK16, nh), per_b(TOK16, nh), per_b(nh, TOK16),
                per_l(CONV_W, conv_dim), per_l(1, conv_dim), full(1, d_inner), full(1, d_inner)]
    args = [h0, xbc, hist, z, dt16, acs16, acst, conv_w, conv_b, dskx, gs]
    if aliased:
        in_specs.append(pl.BlockSpec(memory_space=pl.ANY))
        args.append(prev_states)
    return pl.pallas_call(
        functools.partial(_ssd_sample_kernel, n_new=n_new, aliased=aliased),
        grid=(db,),
        in_specs=in_specs,
        out_specs=[per_b(TOK16, d_inner),
                   pl.BlockSpec((1, 1, hd, n), lambda b: (layer, b, 0, 0))],
        out_shape=[jax.ShapeDtypeStruct((db, TOK16, d_inner), F32),
                   jax.ShapeDtypeStruct((depth, db, hd, n), F32)],
        scratch_shapes=[pltpu.VMEM((SUBLANES, conv_dim), F32),
                        pltpu.VMEM((TOK16, conv_dim), F32),
                        pltpu.VMEM((TOK16, d_inner), F32)],
        input_output_aliases={len(args) - 1: 1} if aliased else {},
        compiler_params=_cparams("parallel"),
        name="ssd_sample",
    )(*args)


def _page_head(page_ref, hk):
    return page_ref[0, 0, pl.ds(hk, PAGE, stride=N_KV_HEADS), :]


def _attn_sample_kernel(pt_ref, q_ref, kn_ref, vn_ref, nb_ref, tm_ref, *refs, n_new, pps):
    del pt_ref
    k_refs, v_refs, lf_refs = refs[0:pps], refs[pps:2 * pps], refs[2 * pps:3 * pps]
    o_ref, qs_ref, knp_ref, m_ref, l_ref, acc_ref, carry_ref, r_ref = refs[3 * pps:]
    b = pl.program_id(0)
    p = pl.program_id(1)
    rows = KV_GROUP * n_new

    @pl.when((b == 0) & (p == 0))
    def _():
        knp_ref[...] = jnp.zeros_like(knp_ref)

    @pl.when(p == 0)
    def _():
        for head in range(N_HEADS):
            qs_ref[head * n_new:(head + 1) * n_new, :] = q_ref[0, :, head * HEAD_DIM:(head + 1) * HEAD_DIM]
        knp_ref[0:n_new, :] = kn_ref[0]
        m_ref[...] = jnp.full_like(m_ref, NEG_INF)
        l_ref[...] = jnp.zeros_like(l_ref)
        acc_ref[...] = jnp.zeros_like(acc_ref)
        carry_ref[...] = jnp.zeros_like(carry_ref)

    tm = tm_ref[...]
    qh = [qs_ref[hk * rows:(hk + 1) * rows, :].astype(BF16) for hk in range(N_KV_HEADS)]
    s_cols = []
    for u in range(pps):
        lft = lf_refs[u][0, 0]
        hi, mid, lo = _split3(lft)
        r16 = _dot(hi, tm) + _dot(mid, tm) + _dot(lo, tm) + carry_ref[...]
        carry_ref[...] = carry_ref[...] + jnp.sum(lft, axis=-1, keepdims=True)
        for head in range(N_HEADS):
            r_ref[head * n_new:(head + 1) * n_new, u * PAGE:(u + 1) * PAGE] = jnp.broadcast_to(
                r16[head:head + 1, :], (n_new, PAGE))
        s_cols.append(jnp.concatenate(
            [_dot_nt(qh[hk], _page_head(k_refs[u], hk).astype(BF16)) for hk in range(N_KV_HEADS)], axis=0))
    s = jnp.concatenate(s_cols, axis=1) + r_ref[...] * LOG2E
    m_old = m_ref[...]
    m_new = jnp.maximum(m_old, jnp.max(s, axis=-1, keepdims=True))
    pr = jnp.exp2(s - m_new)
    alpha = jnp.exp2(m_old - m_new)
    l_ref[...] = alpha * l_ref[...] + jnp.sum(pr, axis=-1, keepdims=True)
    m_ref[...] = m_new
    acc_old = acc_ref[...]
    for hk in range(N_KV_HEADS):
        rs = slice(hk * rows, (hk + 1) * rows)
        upd = alpha[rs, :] * acc_old[rs, :]
        for u in range(pps):
            upd = upd + _dot(pr[rs, u * PAGE:(u + 1) * PAGE].astype(BF16),
                             _page_head(v_refs[u], hk).astype(BF16))
        acc_ref[rs, :] = upd

    @pl.when(p == pl.num_programs(1) - 1)
    def _():
        sn = jnp.concatenate(
            [_dot_nt(qh[hk], knp_ref[:, hk * HEAD_DIM:(hk + 1) * HEAD_DIM].astype(BF16))
             for hk in range(N_KV_HEADS)], axis=0) + nb_ref[0] * LOG2E
        m_old = m_ref[...]
        m_new = jnp.maximum(m_old, jnp.max(sn, axis=-1, keepdims=True))
        pn = jnp.exp2(sn - m_new)
        alpha = jnp.exp2(m_old - m_new)
        l = alpha * l_ref[...] + jnp.sum(pn, axis=-1, keepdims=True)
        acc = alpha * acc_ref[...]
        vn = vn_ref[0]
        for hk in range(N_KV_HEADS):
            rs = slice(hk * rows, (hk + 1) * rows)
            upd = acc[rs, :]
            for j in range(n_new):
                upd = upd + pn[rs, j:j + 1] * vn[j:j + 1, hk * HEAD_DIM:(hk + 1) * HEAD_DIM]
            acc_ref[rs, :] = upd / l[rs, :]
        for head in range(N_HEADS):
            o_ref[0, :, head * HEAD_DIM:(head + 1) * HEAD_DIM] = acc_ref[head * n_new:(head + 1) * n_new, :]


def attn_sample(layer, page_table, q, kn, vn, nb, cache_k, cache_v, cache_lft):
    db, n_new, _ = q.shape
    n_pages = page_table.shape[1]
    pps = PAGES_PER_STEP
    kvw = N_KV_HEADS * HEAD_DIM
    n_rows = n_new * N_HEADS
    idx = np.arange(PAGE)
    tm = jnp.asarray(idx[:, None] > idx[None, :], BF16)
    pt_flat = page_table.reshape(-1)

    def page(u, ndim):
        def index_map(b, p, pt):
            return (layer, pt[b * n_pages + (n_pages - 1 - (p * pps + u))]) + (0,) * ndim
        return index_map

    per_b = lambda *shape: pl.BlockSpec((1,) + shape, lambda b, p, pt: (b,) + (0,) * len(shape))
    kv_spec = lambda u: pl.BlockSpec((1, 1, PAGE * N_KV_HEADS, HEAD_DIM), page(u, 2))
    lf_spec = lambda u: pl.BlockSpec((1, 1, N_HEADS, PAGE), page(u, 2))
    grid_spec = pltpu.PrefetchScalarGridSpec(
        num_scalar_prefetch=1,
        grid=(db, n_pages // pps),
        in_specs=[per_b(n_new, N_HEADS * HEAD_DIM), per_b(n_new, kvw), per_b(n_new, kvw),
                  per_b(n_rows, LANES),
                  pl.BlockSpec((PAGE, PAGE), lambda b, p, pt: (0, 0))]
                 + [kv_spec(u) for u in range(pps)] + [kv_spec(u) for u in range(pps)]
                 + [lf_spec(u) for u in range(pps)],
        out_specs=per_b(n_new, N_HEADS * HEAD_DIM),
        scratch_shapes=[pltpu.VMEM((n_rows, HEAD_DIM), F32),
                        pltpu.VMEM((PAGE, kvw), F32),
                        pltpu.VMEM((n_rows, 1), F32),
                        pltpu.VMEM((n_rows, 1), F32),
                        pltpu.VMEM((n_rows, HEAD_DIM), F32),
                        pltpu.VMEM((N_HEADS, 1), F32),
                        pltpu.VMEM((n_rows, pps * PAGE), F32)],
    )
    return pl.pallas_call(
        functools.partial(_attn_sample_kernel, n_new=n_new, pps=pps),
        grid_spec=grid_spec,
        out_shape=jax.ShapeDtypeStruct((db, n_new, N_HEADS * HEAD_DIM), F32),
        compiler_params=_cparams("arbitrary", "arbitrary"),
        name="attn_sample",
    )(pt_flat, q, kn, vn, nb, tm, *([cache_k] * pps), *([cache_v] * pps), *([cache_lft] * pps))


def kernel(x_prompt, x_sample, cache_k, cache_v, cache_logf, state_ssm, state_conv, page_table,
           p_prompt, p_sample, g_ffn1, w_ffn1_in, w_ffn1_out, g_mix, w_in, b_f, g_q, g_k,
           conv_w, conv_b, dt_bias, a_log, d_skip, g_ssm, w_branch_attn, w_branch_ssm, w_out,
           g_ffn2, w_ffn2_in, w_ffn2_out, g_ple, w_ple_gate, w_ple_proj):
    depth = w_in.shape[0]
    bp, seq, d_model = x_prompt.shape
    db, n_new, _ = x_sample.shape
    mp, ms = bp * seq, db * n_new
    attn_w = N_HEADS * HEAD_DIM
    kv_w = N_KV_HEADS * HEAD_DIM
    d_inner = SSM_GROUPS * GROUP_WIDTH
    conv_dim = d_inner + 2 * SSM_GROUPS * D_STATE
    ssm_heads = d_inner // SSM_HEAD_DIM
    assert KV_GROUP * n_new == SUBLANES and page_table.shape[1] % PAGES_PER_STEP == 0
    o_q, o_k, o_v = 0, attn_w, attn_w + kv_w
    o_f = o_v + kv_w
    o_z = o_f + N_HEADS
    o_x = o_z + d_inner
    o_dt = o_x + conv_dim
    o_ga = o_dt + ssm_heads
    z_col, xbc_col = 0, d_inner
    ga_col, gb_col = 0, d_model
    scale = HEAD_DIM ** -0.5

    n_pool = cache_k.shape[1]
    ck = cache_k.reshape(depth, n_pool, PAGE * N_KV_HEADS, HEAD_DIM)
    cv = cache_v.reshape(depth, n_pool, PAGE * N_KV_HEADS, HEAD_DIM)
    clft = jnp.swapaxes(cache_logf, 2, 3)

    state4 = state_ssm.reshape(depth, db, d_inner, D_STATE)
    conv_b3 = conv_b.reshape(depth, 1, conv_dim)

    h = jnp.concatenate([x_prompt.reshape(mp, d_model), x_sample.reshape(ms, d_model)], axis=0)
    outs = [[] for _ in range(9)]
    states_s = None
    wt = jnp.swapaxes(w_in, 1, 2)
    for i in range(depth):
        gate_bias = jnp.concatenate(
            [dt_bias[i], b_f[i], jnp.zeros((LANES - ssm_heads - N_HEADS,), F32)]).reshape(1, LANES)
        alog_row = jnp.concatenate([a_log[i], jnp.zeros((LANES - ssm_heads,), F32)]).reshape(1, LANES)
        dskx = jnp.repeat(d_skip[i], SSM_HEAD_DIM).reshape(1, d_inner)
        gs = g_ssm[i].reshape(1, d_inner)

        act = mm_swiglu(rmsnorm(h, g_ffn1[i]), w_ffn1_in, i)
        h = mm_residual(act, w_ffn1_out, i, h, 0.5, TM)

        u = rmsnorm(h, g_mix[i])
        (q_b,) = mm_headnorm(u, wt, i, g_q[i], [BF16], o_q, attn_w, mult=scale * LOG2E)
        k_f, k_b = mm_headnorm(u, wt, i, g_k[i], [F32, BF16], o_k, kv_w)
        v_f, v_b = mm_plain(u, wt, i, [F32, BF16], TN, o_v, kv_w)
        (wide,) = mm_plain(u, wt, i, [F32], 2 * TN, o_z, d_inner + conv_dim)
        (gate_pre,) = mm_plain(u, wt, i, [F32], 2 * TN, o_ga, 2 * d_model)
        small = mm_gate_logits(u, wt, i, o_dt, ssm_heads, o_f, N_HEADS)

        dt_p, lf_p, c_p, acs_p = gates(small[:mp], gate_bias, alog_row, seq, SSM_CHUNK)
        dt_s, lf_s, c_s, acs_s = gates(small[mp:], gate_bias, alog_row, n_new, n_new)
        fcols = slice(ssm_heads, ssm_heads + N_HEADS)

        ct = c_p[:, fcols].reshape(bp, seq, N_KV_HEADS, KV_GROUP).transpose(0, 2, 3, 1)
        ct = jnp.pad(ct, ((0, 0), (0, 0), (0, SUBLANES - KV_GROUP), (0, 0)))
        o_attn_p = attn_prompt(q_b, k_b, v_b, ct, bp, seq)
        pg = lambda a: a[:, :ssm_heads].reshape(bp, seq, SSM_GROUPS, HEADS_PER_GROUP).transpose(0, 2, 1, 3)
        acsg = pg(acs_p)
        o_ssm_p, st_p = ssd_prompt(i, wide, z_col, xbc_col, pg(dt_p), acsg, acsg.transpose(0, 1, 3, 2),
                                   conv_w, conv_b3, dskx, gs, bp, seq)

        wide_s = wide[mp:].reshape(db, n_new, -1)
        xbc_s = wide_s[:, :, xbc_col:xbc_col + conv_dim]
        tail = TOK16 - n_new
        dt16 = jnp.pad(dt_s[:, :ssm_heads].reshape(db, n_new, ssm_heads), ((0, 0), (0, tail), (0, 0)))
        acs16 = jnp.pad(acs_s[:, :ssm_heads].reshape(db, n_new, ssm_heads), ((0, 0), (0, tail), (0, 0)),
                        mode="edge")
        o_ssm_s16, states_s = ssd_sample(
            i, depth, state4, xbc_s, state_conv, wide_s[:, :, z_col:z_col + d_inner],
            dt16, acs16, acs16.transpose(0, 2, 1), conv_w, conv_b3, dskx, gs, states_s)
        o_ssm_s = o_ssm_s16[:, :n_new].reshape(ms, d_inner).astype(BF16)

        cn = c_s[:, fcols].reshape(db, n_new, N_HEADS)
        tt = np.arange(n_new)
        nbias = jnp.where((tt[None, :] <= tt[:, None])[None, None, :, :],
                          -cn.transpose(0, 2, 1)[:, :, None, :], NEG_INF)
        nbias = jnp.pad(nbias.reshape(db, N_HEADS * n_new, n_new),
                        ((0, 0), (0, 0), (0, LANES - n_new)), constant_values=NEG_INF)
        o_attn_s = attn_sample(i, page_table,
                               q_b[mp:].astype(F32).reshape(db, n_new, attn_w),
                               k_b[mp:].astype(F32).reshape(db, n_new, kv_w),
                               v_f[mp:].reshape(db, n_new, kv_w),
                               nbias, ck, cv, clft)
        o_attn_s = o_attn_s.reshape(ms, attn_w).astype(BF16)

        merged = mm_merge(o_attn_p, o_attn_s, w_branch_attn, o_ssm_p, o_ssm_s, w_branch_ssm, i,
                          gate_pre, ga_col, gb_col)
        h = mm_residual(merged, w_out, i, h, 1.0, TM_WIDE)

        act = mm_swiglu(rmsnorm(h, g_ffn2[i]), w_ffn2_in, i)
        h = mm_residual(act, w_ffn2_out, i, h, 0.5, TM)

        p_l = jnp.concatenate([p_prompt[i].reshape(mp, -1), p_sample[i].reshape(ms, -1)], axis=0).astype(BF16)
        h = mm_ple(rmsnorm(h, g_ple[i]), w_ple_gate, p_l, w_ple_proj, i, h)

        conv_p = jnp.stack([wide[(b + 1) * seq - (CONV_W - 1):(b + 1) * seq, xbc_col:xbc_col + conv_dim]
                            for b in range(bp)])
        layer_out = (
            k_f[:mp].reshape(bp, seq, N_KV_HEADS, HEAD_DIM), v_f[:mp].reshape(bp, seq, N_KV_HEADS, HEAD_DIM),
            lf_p[:, fcols].reshape(bp, seq, N_HEADS), conv_p,
            st_p.reshape(bp, ssm_heads, SSM_HEAD_DIM, D_STATE),
            k_f[mp:].reshape(db, n_new, N_KV_HEADS, HEAD_DIM), v_f[mp:].reshape(db, n_new, N_KV_HEADS, HEAD_DIM),
            lf_s[:, fcols].reshape(db, n_new, N_HEADS),
            jnp.concatenate([state_conv[i], xbc_s], axis=1)[:, n_new:n_new + CONV_W - 1])
        for lst, val in zip(outs, layer_out):
            lst.append(val)

    stacked = [jnp.stack(lst) for lst in outs]
    return (h[:mp].reshape(bp, seq, d_model), h[mp:].reshape(db, n_new, d_model), *stacked,
            states_s.reshape(depth, db, ssm_heads, SSM_HEAD_DIM, D_STATE))
```
